```python
import math
import jax, jax.numpy as jnp
from jax import lax
import numpy as np


D_MODEL = 4096
BATCH = 4
SEQ = 2048
DEPTH = 4

N_MIXERS = 3
N_NSA = (DEPTH + 2) // 3
N_RG = (DEPTH + 1) // 3
N_HG = DEPTH // 3
NORM_EPS = 1e-6

NSA_HEADS = 32
NSA_KV_GROUPS = 4
NSA_Q_PER_GROUP = NSA_HEADS // NSA_KV_GROUPS
NSA_HEAD_DIM = D_MODEL // NSA_HEADS
NSA_INNER = NSA_HEADS * NSA_HEAD_DIM
NSA_KV = NSA_KV_GROUPS * NSA_HEAD_DIM
NSA_IN = NSA_INNER + 6 * NSA_KV + 3 * NSA_HEADS + NSA_INNER
CMP_LEN = 32
CMP_STRIDE = 16
SEL_LEN = 64
SEL_TOPK = 16
WINDOW = 512
SEL_QCHUNK = 16
WIN_QBLOCK = 128
ALIBI_MAX_EXP = 8.0
NEG_INF = -1e30
FORCE_SCORE = 1e6

RG_WIDTH = D_MODEL
RG_BLOCKS = 16
RG_BLOCK = RG_WIDTH // RG_BLOCKS
RG_CONV = 4
RG_C = 8.0

HG_HEADS = 32
HG_KEY_DIM = 128
HG_VAL_DIM = D_MODEL // HG_HEADS
HG_KEY = HG_HEADS * HG_KEY_DIM
HG_VAL = HG_HEADS * HG_VAL_DIM
HG_IN = 2 * HG_KEY + 2 * HG_VAL
HG_CHUNK = 64

kernel_name = 'hybrid_nsa_rglru_hgrn2_sandwich'


def _rmsnorm(x, gain):
    xf = x.astype(jnp.float32)
    y = xf * lax.rsqrt(jnp.mean(xf * xf, axis=-1, keepdims=True) + NORM_EPS)
    return (y * gain.astype(jnp.float32)).astype(x.dtype)


def _split(t, sizes):
    cuts = [int(c) for c in np.cumsum(sizes)[:-1]]
    return jnp.split(t, cuts, axis=-1)


def _alibi_slopes():
    return 2.0 ** (-ALIBI_MAX_EXP * jnp.arange(1, NSA_HEADS + 1, dtype=jnp.float32) / NSA_HEADS)


def _compress(t, pe, w1, w2):
    bsz, seq, g, d = t.shape
    ratio = CMP_LEN // CMP_STRIDE
    n_chunk = seq // CMP_STRIDE
    n_cmp = n_chunk - ratio + 1
    chunks = t.reshape(bsz, n_chunk, CMP_STRIDE, g, d)
    blocks = jnp.concatenate([chunks[:, r:r + n_cmp] for r in range(ratio)], axis=2)
    blocks = blocks + pe[None, None, :, None, :].astype(t.dtype)
    flat = blocks.transpose(0, 1, 3, 2, 4).reshape(bsz, n_cmp, g, CMP_LEN * d)
    return jax.nn.gelu(flat @ w1) @ w2


def _nsa_select(q, ks, vs, p_cmp, slopes, pos, scale):
    bsz, seq, g, hg, d = q.shape
    c_ratio = CMP_LEN // CMP_STRIDE
    s_ratio = SEL_LEN // CMP_STRIDE
    n_sel = seq // SEL_LEN
    pg = p_cmp.sum(axis=2)
    padded = jnp.pad(pg, ((0, 0), (0, 0), (0, 0), (c_ratio - 1, c_ratio - 1)))
    span = s_ratio * (n_sel - 1) + 1
    p_slc = sum(padded[..., m + n:m + n + span:s_ratio] for m in range(s_ratio) for n in range(c_ratio))
    blk = jnp.arange(n_sel)[None, :]
    cur = (jnp.arange(seq) // SEL_LEN)[:, None]
    forced = (blk == 0) | (blk == cur) | (blk == cur - 1)
    future = blk > cur
    score = jnp.where(forced, FORCE_SCORE, jnp.where(future, -1.0, p_slc))
    n_top = min(SEL_TOPK, n_sel)
    _, idx = lax.top_k(score, n_top)

    k_blocks = ks.reshape(bsz, n_sel, SEL_LEN, g, d).transpose(0, 3, 1, 2, 4)
    v_blocks = vs.reshape(bsz, n_sel, SEL_LEN, g, d).transpose(0, 3, 1, 2, 4)
    n_q = seq // SEL_QCHUNK
    q_ch = q.transpose(0, 2, 1, 3, 4).reshape(bsz, g, n_q, SEL_QCHUNK, hg, d).transpose(2, 0, 1, 3, 4, 5)
    idx_ch = idx.reshape(bsz, g, n_q, SEL_QCHUNK, n_top).transpose(2, 0, 1, 3, 4)
    t_ch = pos.reshape(n_q, SEL_QCHUNK)
    gather = jax.vmap(jax.vmap(lambda blocks, ix: blocks[ix]))
    offs = jnp.arange(SEL_LEN)
    n_keys = n_top * SEL_LEN

    def sel_chunk(args):
        qq, ii, tt = args
        kk = gather(k_blocks, ii).reshape(bsz, g, SEL_QCHUNK, n_keys, d)
        vv = gather(v_blocks, ii).reshape(bsz, g, SEL_QCHUNK, n_keys, d)
        kpos = (ii[..., None] * SEL_LEN + offs).reshape(bsz, g, SEL_QCHUNK, n_keys).astype(jnp.float32)
        dist = (tt[None, None, :, None] - kpos)[:, :, :, None, :]
        s = jnp.einsum('bgqhd,bgqkd->bgqhk', qq, kk).astype(jnp.float32) * scale
        s = s - slopes[None, :, None, :, None] * dist
        s = jnp.where(dist >= 0, s, NEG_INF)
        pr = jax.nn.softmax(s, axis=-1)
        return jnp.einsum('bgqhk,bgqkd->bgqhd', pr.astype(vv.dtype), vv)

    o = lax.map(sel_chunk, (q_ch, idx_ch, t_ch))
    return o.transpose(1, 0, 3, 2, 4, 5).reshape(bsz, seq, g, hg, d)


def _nsa_window(q, kw, vw, slopes, scale):
    bsz, seq, g, hg, d = q.shape
    n_blk = seq // WIN_QBLOCK
    kw_pad = jnp.pad(kw, ((0, 0), (WINDOW, 0), (0, 0), (0, 0)))
    vw_pad = jnp.pad(vw, ((0, 0), (WINDOW, 0), (0, 0), (0, 0)))
    q_blk = q.reshape(bsz, n_blk, WIN_QBLOCK, g, hg, d).transpose(1, 0, 2, 3, 4, 5)
    span = WIN_QBLOCK + WINDOW

    def win_block(args):
        c, qq = args
        start = c * WIN_QBLOCK
        kk = lax.dynamic_slice_in_dim(kw_pad, start, span, axis=1)
        vv = lax.dynamic_slice_in_dim(vw_pad, start, span, axis=1)
        tq = (start + jnp.arange(WIN_QBLOCK)).astype(jnp.float32)
        tk = (start - WINDOW + jnp.arange(span)).astype(jnp.float32)
        dist = tq[:, None] - tk[None, :]
        valid = (dist >= 0) & (dist < WINDOW) & (tk[None, :] >= 0)
        s = jnp.einsum('bqghd,bkgd->bghqk', qq, kk).astype(jnp.float32) * scale
        s = jnp.where(valid, s - slopes[None, :, :, None, None] * dist, NEG_INF)
        pr = jax.nn.softmax(s, axis=-1)
        return jnp.einsum('bghqk,bkgd->bqghd', pr.astype(vv.dtype), vv)

    o = lax.map(win_block, (jnp.arange(n_blk), q_blk))
    return o.transpose(1, 0, 2, 3, 4, 5).reshape(bsz, seq, g, hg, d)


def _nsa_mixer(h, w_in, cmp_pe, cmp_w1, cmp_w2, w_out):
    bsz, seq, _ = h.shape
    f32 = jnp.float32
    scale = NSA_HEAD_DIM ** -0.5
    q, kc, vc, ks, vs, kw, vw, gl, z = _split(h @ w_in, [NSA_INNER] + [NSA_KV] * 6 + [3 * NSA_HEADS, NSA_INNER])
    q = q.reshape(bsz, seq, NSA_KV_GROUPS, NSA_Q_PER_GROUP, NSA_HEAD_DIM)
    kc, vc, ks, vs, kw, vw = [t.reshape(bsz, seq, NSA_KV_GROUPS, NSA_HEAD_DIM) for t in (kc, vc, ks, vs, kw, vw)]
    slopes = _alibi_slopes().reshape(NSA_KV_GROUPS, NSA_Q_PER_GROUP)
    pos = jnp.arange(seq, dtype=f32)

    k_cmp = _compress(kc, cmp_pe[0], cmp_w1[0], cmp_w2[0])
    v_cmp = _compress(vc, cmp_pe[1], cmp_w1[1], cmp_w2[1])
    n_cmp = k_cmp.shape[1]
    cmp_end = jnp.arange(n_cmp, dtype=f32) * CMP_STRIDE + (CMP_LEN - 1)
    dist = pos[:, None] - cmp_end[None, :]
    valid = dist >= 0
    s = jnp.einsum('bsgjd,bngd->bgjsn', q, k_cmp).astype(f32) * scale - slopes[None, :, :, None, None] * dist
    s = jnp.where(valid, s, NEG_INF)
    p_cmp = jax.nn.softmax(s, axis=-1) * jnp.any(valid, axis=-1)[:, None].astype(f32)
    o_cmp = jnp.einsum('bgjsn,bngd->bsgjd', p_cmp.astype(v_cmp.dtype), v_cmp)

    o_slc = _nsa_select(q, ks, vs, p_cmp, slopes, pos, scale)
    o_win = _nsa_window(q, kw, vw, slopes, scale)

    g = jax.nn.sigmoid(gl.astype(f32)).reshape(bsz, seq, 3, NSA_KV_GROUPS, NSA_Q_PER_GROUP, 1)
    o = g[:, :, 0] * o_cmp + g[:, :, 1] * o_slc + g[:, :, 2] * o_win
    y = o.reshape(bsz, seq, NSA_INNER).astype(h.dtype) * jax.nn.silu(z)
    return y @ w_out


def _lin_rec_combine(left, right):
    a_l, b_l = left
    a_r, b_r = right
    return (a_l * a_r, a_r * b_l + b_r)


def _rglru_mixer(h, w_in, conv_w, conv_b, gate_w, gate_b, lam, w_out):
    bsz, seq, _ = h.shape
    f32 = jnp.float32
    xb, z = _split(h @ w_in, [RG_WIDTH, RG_WIDTH])
    xb = lax.conv_general_dilated(xb, conv_w[:, None, :].astype(xb.dtype), window_strides=(1,), padding=[(RG_CONV - 1, 0)], dimension_numbers=('NWC', 'WIO', 'NWC'), feature_group_count=RG_WIDTH) + conv_b.astype(xb.dtype)
    xg = xb.reshape(bsz, seq, RG_BLOCKS, RG_BLOCK)
    gates = jax.nn.sigmoid((jnp.einsum('bsnc,knce->kbsne', xg, gate_w) + gate_b[:, None, None]).astype(f32)).reshape(2, bsz, seq, RG_WIDTH)
    i_gate, r_gate = gates[0], gates[1]
    log_a = -RG_C * r_gate * jax.nn.softplus(-lam.astype(f32))
    a = jnp.exp(log_a)
    mult = jnp.sqrt(-jnp.expm1(2.0 * log_a))
    mult = jnp.where((jnp.arange(seq) == 0)[None, :, None], 1.0, mult)
    u = mult * i_gate * xb.astype(f32)
    _, hs = lax.associative_scan(_lin_rec_combine, (a, u), axis=1)
    y = hs.astype(h.dtype) * jax.nn.silu(z)
    return y @ w_out


def _hgrn2_chunk_scan(q, k, v, log_f):
    bsz, nh, seq, dk = q.shape
    dv = v.shape[-1]
    n_c = seq // HG_CHUNK

    def to_chunks(t):
        return t.reshape(bsz, nh, n_c, HG_CHUNK, t.shape[-1]).transpose(2, 0, 1, 3, 4)

    causal = jnp.tril(jnp.ones((HG_CHUNK, HG_CHUNK), dtype=bool))[:, :, None]

    def step(state, inp):
        qc, kc, vc, gc = inp
        b = jnp.cumsum(gc, axis=2)
        o_inter = jnp.einsum('bhtd,bhdv->bhtv', qc * jnp.exp(b), state)
        rel = jnp.where(causal, b[:, :, :, None, :] - b[:, :, None, :, :], -jnp.inf)
        att = jnp.einsum('bhtd,bhsd,bhtsd->bhts', qc, kc, jnp.exp(rel))
        o = o_inter + jnp.einsum('bhts,bhsv->bhtv', att, vc)
        b_last = b[:, :, -1:, :]
        new_state = jnp.exp(b_last[:, :, 0, :])[..., None] * state + jnp.einsum('bhsd,bhsv->bhdv', kc * jnp.exp(b_last - b), vc)
        return new_state, o

    init = jnp.zeros((bsz, nh, dk, dv), jnp.float32)
    _, o = lax.scan(step, init, (to_chunks(q), to_chunks(k), to_chunks(v), to_chunks(log_f)))
    return o.transpose(1, 0, 3, 2, 4).reshape(bsz, seq, nh, dv)


def _hgrn2_mixer(h, w_in, lb, norm_gain, w_out):
    bsz, seq, _ = h.shape
    f32 = jnp.float32
    q, f, v, g = _split(h @ w_in, [HG_KEY, HG_KEY, HG_VAL, HG_VAL])
    q = jax.nn.silu(q.astype(f32))
    f = lb + (1.0 - lb) * jax.nn.sigmoid(f.astype(f32))
    log_f = jnp.log(f)
    k = 1.0 - f
    heads_k = lambda t: t.reshape(bsz, seq, HG_HEADS, HG_KEY_DIM).transpose(0, 2, 1, 3)
    vh = v.astype(f32).reshape(bsz, seq, HG_HEADS, HG_VAL_DIM).transpose(0, 2, 1, 3)
    o = _hgrn2_chunk_scan(heads_k(q), heads_k(k), vh, heads_k(log_f))
    o = _rmsnorm(o, norm_gain) * jax.nn.silu(g.astype(f32).reshape(bsz, seq, HG_HEADS, HG_VAL_DIM))
    return o.reshape(bsz, seq, HG_VAL).astype(h.dtype) @ w_out


def setup_inputs(seed: int = 0) -> dict:
    key = jax.random.key(seed)
    ks = jax.random.split(key, 20)
    f32 = jnp.float32

    def nrm(k, shape, scale):
        return jax.random.normal(k, shape, f32) * scale

    x = nrm(ks[0], (BATCH, SEQ, D_MODEL), 1.0)
    pre_norm_gain = 1.0 + nrm(ks[1], (DEPTH, D_MODEL), 0.02)
    post_norm_gain = 1.0 + nrm(ks[2], (DEPTH, D_MODEL), 0.02)
    nsa_w_in = nrm(ks[3], (N_NSA, D_MODEL, NSA_IN), D_MODEL ** -0.5)
    nsa_cmp_pe = nrm(ks[4], (N_NSA, 2, CMP_LEN, NSA_HEAD_DIM), 0.1)
    nsa_cmp_w1 = nrm(ks[5], (N_NSA, 2, CMP_LEN * NSA_HEAD_DIM, NSA_HEAD_DIM), (CMP_LEN * NSA_HEAD_DIM) ** -0.5)
    nsa_cmp_w2 = nrm(ks[6], (N_NSA, 2, NSA_HEAD_DIM, NSA_HEAD_DIM), NSA_HEAD_DIM ** -0.5)
    nsa_w_out = nrm(ks[7], (N_NSA, NSA_INNER, D_MODEL), NSA_INNER ** -0.5)
    rg_w_in = nrm(ks[8], (N_RG, D_MODEL, 2 * RG_WIDTH), D_MODEL ** -0.5)
    rg_conv_w = nrm(ks[9], (N_RG, RG_CONV, RG_WIDTH), RG_CONV ** -0.5)
    rg_conv_b = nrm(ks[10], (N_RG, RG_WIDTH), 0.01)
    rg_gate_w = nrm(ks[11], (N_RG, 2, RG_BLOCKS, RG_BLOCK, RG_BLOCK), RG_BLOCK ** -0.5)
    rg_gate_b = nrm(ks[12], (N_RG, 2, RG_BLOCKS, RG_BLOCK), 0.01)
    u = jax.random.uniform(ks[13], (N_RG, RG_WIDTH), f32, 0.9, 0.999)
    log_a = jnp.log(u) / RG_C
    rg_lambda = log_a - jnp.log(-jnp.expm1(log_a))
    rg_w_out = nrm(ks[14], (N_RG, RG_WIDTH, D_MODEL), RG_WIDTH ** -0.5)
    hg_w_in = nrm(ks[15], (N_HG, D_MODEL, HG_IN), D_MODEL ** -0.5)
    hg_lb_logits = nrm(ks[16], (DEPTH, HG_KEY), 0.1)
    hg_norm_gain = 1.0 + nrm(ks[17], (N_HG, HG_VAL_DIM), 0.02)
    hg_w_out = nrm(ks[18], (N_HG, HG_VAL, D_MODEL), HG_VAL ** -0.5)
    return {'x': x, 'pre_norm_gain': pre_norm_gain, 'post_norm_gain': post_norm_gain,
            'nsa_w_in': nsa_w_in, 'nsa_cmp_pe': nsa_cmp_pe, 'nsa_cmp_w1': nsa_cmp_w1, 'nsa_cmp_w2': nsa_cmp_w2, 'nsa_w_out': nsa_w_out,
            'rg_w_in': rg_w_in, 'rg_conv_w': rg_conv_w, 'rg_conv_b': rg_conv_b, 'rg_gate_w': rg_gate_w, 'rg_gate_b': rg_gate_b,
            'rg_lambda': rg_lambda, 'rg_w_out': rg_w_out,
            'hg_w_in': hg_w_in, 'hg_lb_logits': hg_lb_logits, 'hg_norm_gain': hg_norm_gain, 'hg_w_out': hg_w_out}


def reference(x, pre_norm_gain, post_norm_gain, nsa_w_in, nsa_cmp_pe, nsa_cmp_w1, nsa_cmp_w2, nsa_w_out,
              rg_w_in, rg_conv_w, rg_conv_b, rg_gate_w, rg_gate_b, rg_lambda, rg_w_out,
              hg_w_in, hg_lb_logits, hg_norm_gain, hg_w_out):
    lb_all = jax.nn.softmax(hg_lb_logits.astype(jnp.float32), axis=0)
    lb_all = jnp.cumsum(lb_all, axis=0) - lb_all[0]
    for i in range(DEPTH):
        h = _rmsnorm(x, pre_norm_gain[i])
        kind, j = i % N_MIXERS, i // N_MIXERS
        if kind == 0:
            y = _nsa_mixer(h, nsa_w_in[j], nsa_cmp_pe[j], nsa_cmp_w1[j], nsa_cmp_w2[j], nsa_w_out[j])
        elif kind == 1:
            y = _rglru_mixer(h, rg_w_in[j], rg_conv_w[j], rg_conv_b[j], rg_gate_w[j], rg_gate_b[j], rg_lambda[j], rg_w_out[j])
        else:
            y = _hgrn2_mixer(h, hg_w_in[j], lb_all[i], hg_norm_gain[j], hg_w_out[j])
        x = x + _rmsnorm(y, post_norm_gain[i])
    return x
```

```python
import functools
import math

import jax
import jax.numpy as jnp
from jax import lax
from jax.experimental import pallas as pl
from jax.experimental.pallas import tpu as pltpu

F32 = jnp.float32
BF16 = jnp.bfloat16

NORM_EPS = 1e-6
NEG_INF = -1e30
FORCE_SCORE = 1e6

LANES = 128
SUBLANES = 8
V7X_VMEM_LIMIT_BYTES = 56 * 1024 * 1024

NSA_HEADS = 32
NSA_GROUPS = 4
NSA_HPG = NSA_HEADS // NSA_GROUPS
HEAD_DIM = 128
CMP_LEN = 32
CMP_STRIDE = 16
SEL_LEN = 64
SEL_SHIFT = 6
SEL_TOPK = 16
WINDOW = 512
ALIBI_MAX_EXP = 8.0

RG_BLOCKS = 16
RG_CONV = 4
RG_C = 8.0

HG_HEADS = 32
HG_CHUNK = 64
HG_SUB = 16
HG_SUB_SHIFT = 4


def _cparams(sem):
    return pltpu.CompilerParams(dimension_semantics=sem, vmem_limit_bytes=V7X_VMEM_LIMIT_BYTES)


def _sigmoid(x):
    return 1.0 / (1.0 + jnp.exp(-x))


def _silu(x):
    return x * _sigmoid(x)


def _prenorm_kernel(x_ref, g_ref, o_ref):
    x = x_ref[...]
    ms = jnp.mean(x * x, axis=-1, keepdims=True)
    o_ref[...] = (x * lax.rsqrt(ms + NORM_EPS) * g_ref[...]).astype(o_ref.dtype)


def _prenorm(x2d, gain, tm=256):
    t, d = x2d.shape
    return pl.pallas_call(
        _prenorm_kernel,
        out_shape=jax.ShapeDtypeStruct((t, d), BF16),
        grid=(t // tm,),
        in_specs=[pl.BlockSpec((tm, d), lambda i: (i, 0)), pl.BlockSpec((1, d), lambda i: (0, 0))],
        out_specs=pl.BlockSpec((tm, d), lambda i: (i, 0)),
        compiler_params=_cparams(("parallel",)),
        name="prenorm",
    )(x2d, gain.reshape(1, d).astype(F32))


def _postnorm_kernel(x_ref, y_ref, g_ref, o_ref):
    y = y_ref[...].astype(F32)
    ms = jnp.mean(y * y, axis=-1, keepdims=True)
    o_ref[...] = x_ref[...] + y * lax.rsqrt(ms + NORM_EPS) * g_ref[...]


def _postnorm_residual(x2d, y2d, gain, tm=256):
    t, d = x2d.shape
    return pl.pallas_call(
        _postnorm_kernel,
        out_shape=jax.ShapeDtypeStruct((t, d), F32),
        grid=(t // tm,),
        in_specs=[pl.BlockSpec((tm, d), lambda i: (i, 0)), pl.BlockSpec((tm, d), lambda i: (i, 0)),
                  pl.BlockSpec((1, d), lambda i: (0, 0))],
        out_specs=pl.BlockSpec((tm, d), lambda i: (i, 0)),
        compiler_params=_cparams(("parallel",)),
        name="postnorm_residual",
    )(x2d, y2d, gain.reshape(1, d).astype(F32))


def _matmul_kernel(a_ref, b_ref, o_ref, acc_ref):
    k = pl.program_id(2)

    @pl.when(k == 0)
    def _():
        acc_ref[...] = jnp.zeros_like(acc_ref)

    acc_ref[...] += jnp.dot(a_ref[...], b_ref[...], preferred_element_type=F32)

    @pl.when(k == pl.num_programs(2) - 1)
    def _():
        o_ref[...] = acc_ref[...].astype(o_ref.dtype)


def _matmul(a, b, out_dtype, tm=1024, tn=1024, tk=1024):
    m, kdim = a.shape
    _, n = b.shape
    tm, tn, tk = min(tm, m), min(tn, n), min(tk, kdim)
    assert m % tm == 0 and n % tn == 0 and kdim % tk == 0
    return pl.pallas_call(
        _matmul_kernel,
        out_shape=jax.ShapeDtypeStruct((m, n), out_dtype),
        grid=(m // tm, n // tn, kdim // tk),
        in_specs=[pl.BlockSpec((tm, tk), lambda i, j, k: (i, k)), pl.BlockSpec((tk, tn), lambda i, j, k: (k, j))],
        out_specs=pl.BlockSpec((tm, tn), lambda i, j, k: (i, j)),
        scratch_shapes=[pltpu.VMEM((tm, tn), F32)],
        compiler_params=_cparams(("parallel", "parallel", "arbitrary")),
        name="projection",
    )(a, b)


def _gelu_tanh(x):
    return 0.5 * x * (1.0 + jnp.tanh(math.sqrt(2.0 / math.pi) * (x + 0.044715 * (x * x * x))))


def _compress_kernel(x_ref, pe_ref, w1_ref, w2_ref, o_ref, xf_ref):
    s = x_ref.shape[0]
    nch = s // CMP_STRIDE
    xf_ref[...] = x_ref[...].astype(F32)
    acc_lo = jnp.zeros((nch, HEAD_DIM), F32)
    acc_hi = jnp.zeros((nch, HEAD_DIM), F32)
    for l in range(CMP_STRIDE):
        xl = xf_ref[pl.ds(l, nch, stride=CMP_STRIDE), :]
        lo_in = (xl + pe_ref[l:l + 1, :]).astype(BF16)
        hi_in = (xl + pe_ref[CMP_STRIDE + l:CMP_STRIDE + l + 1, :]).astype(BF16)
        acc_lo += jnp.dot(lo_in, w1_ref[l * HEAD_DIM:(l + 1) * HEAD_DIM, :], preferred_element_type=F32)
        acc_hi += jnp.dot(hi_in, w1_ref[(CMP_STRIDE + l) * HEAD_DIM:(CMP_STRIDE + l + 1) * HEAD_DIM, :],
                          preferred_element_type=F32)
    pre = acc_lo + pltpu.roll(acc_hi, nch - 1, 0)
    hid = _gelu_tanh(pre).astype(BF16)
    o_ref[...] = jnp.dot(hid, w2_ref[...], preferred_element_type=F32).astype(o_ref.dtype)


def _nsa_compress(qkv, pe, w1, w2):
    bsz, s, _ = qkv.shape
    nch = s // CMP_STRIDE
    kv_block0 = NSA_HEADS
    return pl.pallas_call(
        _compress_kernel,
        out_shape=jax.ShapeDtypeStruct((bsz, 2, NSA_GROUPS, nch, HEAD_DIM), BF16),
        grid=(bsz, 2, NSA_GROUPS),
        in_specs=[
            pl.BlockSpec((None, s, HEAD_DIM), lambda b, w, g: (b, 0, kv_block0 + w * NSA_GROUPS + g)),
            pl.BlockSpec((None, CMP_LEN, HEAD_DIM), lambda b, w, g: (w, 0, 0)),
            pl.BlockSpec((None, CMP_LEN * HEAD_DIM, HEAD_DIM), lambda b, w, g: (w, 0, 0)),
            pl.BlockSpec((None, HEAD_DIM, HEAD_DIM), lambda b, w, g: (w, 0, 0)),
        ],
        out_specs=pl.BlockSpec((None, None, None, nch, HEAD_DIM), lambda b, w, g: (b, w, g, 0, 0)),
        scratch_shapes=[pltpu.VMEM((s, HEAD_DIM), F32)],
        compiler_params=_cparams(("parallel", "parallel", "parallel")),
        name="nsa_compress",
    )(qkv, pe.astype(F32), w1.astype(BF16), w2.astype(BF16))


def _split3(x):
    hi = x.astype(BF16)
    r1 = x - hi.astype(F32)
    mid = r1.astype(BF16)
    lo = (r1 - mid.astype(F32)).astype(BF16)
    return hi, mid, lo


def _softmax_rows(s):
    m = jnp.max(s, axis=-1, keepdims=True)
    p = jnp.exp(s - m)
    return p, jnp.sum(p, axis=-1, keepdims=True)


def _nsa_attn_kernel(slopes_ref, q_ref, kcmp_ref, vcmp_ref, ks_ref, vs_ref, kw_ref, vw_ref, gl_ref, z_ref, y_ref,
                     *, tq, seq):
    qi = pl.program_id(2)
    t0 = qi * tq
    scale = HEAD_DIM ** -0.5
    n_cmp_pad = kcmp_ref.shape[0]
    nt = (((1,), (1,)), ((), ()))

    tpos_i = t0 + lax.broadcasted_iota(jnp.int32, (tq, 1), 0)
    tpos = tpos_i.astype(F32)

    cmp_end = (lax.broadcasted_iota(jnp.int32, (1, n_cmp_pad), 1) * CMP_STRIDE + (CMP_LEN - 1)).astype(F32)
    dist_c = tpos - cmp_end
    valid_c = dist_c >= 0.0
    row_valid = tpos >= float(CMP_LEN - 1)
    kcmp = kcmp_ref[...]
    vcmp = vcmp_ref[...]
    pg = jnp.zeros((tq, n_cmp_pad), F32)
    o_cmp = []
    for j in range(NSA_HPG):
        qh = q_ref[:, j * HEAD_DIM:(j + 1) * HEAD_DIM]
        sl = slopes_ref[j:j + 1, :]
        s = lax.dot_general(qh, kcmp, nt, preferred_element_type=F32) * scale - sl * dist_c
        s = jnp.where(valid_c, s, NEG_INF)
        p, l = _softmax_rows(s)
        p = jnp.where(valid_c & row_valid, p * (1.0 / l), 0.0)
        pg = pg + p
        o_cmp.append(jnp.dot(p.astype(BF16), vcmp, preferred_element_type=F32))

    n_idx = lax.broadcasted_iota(jnp.int32, (n_cmp_pad, LANES), 0)
    j_idx = lax.broadcasted_iota(jnp.int32, (n_cmp_pad, LANES), 1)
    dd = n_idx - (SEL_LEN // CMP_STRIDE) * j_idx + (CMP_LEN // CMP_STRIDE - 1)
    pool = jnp.where((dd == 0) | (dd == 4), 1.0, jnp.where((dd >= 1) & (dd <= 3), 2.0, 0.0)).astype(BF16)
    p_slc = jnp.zeros((tq, LANES), F32)
    for part in _split3(pg):
        p_slc = p_slc + jnp.dot(part, pool, preferred_element_type=F32)

    n_sel = seq // SEL_LEN
    blk = lax.broadcasted_iota(jnp.int32, (1, LANES), 1)
    cur = lax.shift_right_logical(tpos_i, SEL_SHIFT)
    forced = (blk == 0) | (blk == cur) | (blk == cur - 1)
    future = blk > cur
    score = jnp.where(forced, FORCE_SCORE, jnp.where(future, -1.0, p_slc))
    score = jnp.where(blk < n_sel, score, -2.0)
    rank = jnp.zeros((tq, LANES), F32)
    for jp in range(n_sel):
        col = score[:, jp:jp + 1]
        ahead = (col > score) | ((col == score) & (blk > jp))
        rank = rank + jnp.where(ahead, 1.0, 0.0)
    sel = jnp.where((rank < float(min(SEL_TOPK, n_sel))) & (blk < n_sel), 1.0, 0.0).astype(BF16)
    e_row = lax.broadcasted_iota(jnp.int32, (LANES, seq), 0)
    e_key = lax.broadcasted_iota(jnp.int32, (LANES, seq), 1)
    expand = jnp.where(lax.shift_right_logical(e_key, SEL_SHIFT) == e_row, 1.0, 0.0).astype(BF16)
    selk = jnp.dot(sel, expand, preferred_element_type=F32)

    kpos = lax.broadcasted_iota(jnp.int32, (1, seq), 1).astype(F32)
    dist_s = tpos - kpos
    mask_s = (selk > 0.5) & (dist_s >= 0.0)

    span = tq + WINDOW
    start = pl.multiple_of(jnp.maximum(t0 - WINDOW, 0), tq)
    kposw = (start + lax.broadcasted_iota(jnp.int32, (1, span), 1)).astype(F32)
    dist_w = tpos - kposw
    mask_w = (dist_w >= 0.0) & (dist_w < float(WINDOW))
    kw = kw_ref[pl.ds(start, span), :]
    vw = vw_ref[pl.ds(start, span), :]
    ks = ks_ref[...]
    vs = vs_ref[...]

    gates = _sigmoid(gl_ref[...].astype(F32))
    for j in range(NSA_HPG):
        qh = q_ref[:, j * HEAD_DIM:(j + 1) * HEAD_DIM]
        sl = slopes_ref[j:j + 1, 0:1]
        s = lax.dot_general(qh, ks, nt, preferred_element_type=F32) * scale - sl * dist_s
        p, l = _softmax_rows(jnp.where(mask_s, s, NEG_INF))
        o_slc = jnp.dot(p.astype(BF16), vs, preferred_element_type=F32) * (1.0 / l)

        s = lax.dot_general(qh, kw, nt, preferred_element_type=F32) * scale - sl * dist_w
        p, l = _softmax_rows(jnp.where(mask_w, s, NEG_INF))
        o_win = jnp.dot(p.astype(BF16), vw, preferred_element_type=F32) * (1.0 / l)

        o = (gates[:, j:j + 1] * o_cmp[j] + gates[:, NSA_HPG + j:NSA_HPG + j + 1] * o_slc
             + gates[:, 2 * NSA_HPG + j:2 * NSA_HPG + j + 1] * o_win)
        zj = z_ref[:, j * HEAD_DIM:(j + 1) * HEAD_DIM].astype(F32)
        y_ref[:, j * HEAD_DIM:(j + 1) * HEAD_DIM] = (o * _silu(zj)).astype(y_ref.dtype)


def _alibi_slope_table():
    slopes = 2.0 ** (-ALIBI_MAX_EXP * jnp.arange(1, NSA_HEADS + 1, dtype=F32) / NSA_HEADS)
    return jnp.broadcast_to(slopes.reshape(NSA_GROUPS, NSA_HPG, 1), (NSA_GROUPS, NSA_HPG, LANES))


def _nsa_attention(qkv, kv_cmp, gl, z, tq=256):
    bsz, s, _ = qkv.shape
    gw = NSA_HPG * HEAD_DIM
    ncp = kv_cmp.shape[3]
    kvb = NSA_HEADS

    def kv_spec(which):
        return pl.BlockSpec((None, s, HEAD_DIM), lambda b, g, i: (b, 0, kvb + which * NSA_GROUPS + g))

    return pl.pallas_call(
        functools.partial(_nsa_attn_kernel, tq=tq, seq=s),
        out_shape=jax.ShapeDtypeStruct((bsz, s, NSA_HEADS * HEAD_DIM), BF16),
        grid=(bsz, NSA_GROUPS, s // tq),
        in_specs=[
            pl.BlockSpec((None, NSA_HPG, LANES), lambda b, g, i: (g, 0, 0)),
            pl.BlockSpec((None, tq, gw), lambda b, g, i: (b, i, g)),
            pl.BlockSpec((None, None, None, ncp, HEAD_DIM), lambda b, g, i: (b, 0, g, 0, 0)),
            pl.BlockSpec((None, None, None, ncp, HEAD_DIM), lambda b, g, i: (b, 1, g, 0, 0)),
            kv_spec(2), kv_spec(3), kv_spec(4), kv_spec(5),
            pl.BlockSpec((None, tq, LANES), lambda b, g, i: (b, i, g)),
            pl.BlockSpec((None, tq, gw), lambda b, g, i: (b, i, g)),
        ],
        out_specs=pl.BlockSpec((None, tq, gw), lambda b, g, i: (b, i, g)),
        compiler_params=_cparams(("parallel", "parallel", "parallel")),
        name="nsa_attention",
    )(_alibi_slope_table(), qkv, kv_cmp, kv_cmp, qkv, qkv, qkv, qkv, gl, z)


def _nsa_mixer(h2d, bsz, seq, w_in, cmp_pe, cmp_w1, cmp_w2, w_out):
    d = h2d.shape[1]
    inner = NSA_HEADS * HEAD_DIM
    kv = NSA_GROUPS * HEAD_DIM
    n_qkv = inner + 6 * kv
    n_gl = 3 * NSA_HEADS
    w_qkv = w_in[:, :n_qkv].astype(BF16)
    w_z = w_in[:, n_qkv + n_gl:].astype(BF16)
    w_gl = w_in[:, n_qkv:n_qkv + n_gl].reshape(d, 3, NSA_GROUPS, NSA_HPG).transpose(0, 2, 1, 3)
    w_gl = w_gl.reshape(d, NSA_GROUPS, 3 * NSA_HPG)
    w_gl = jnp.pad(w_gl, ((0, 0), (0, 0), (0, LANES - 3 * NSA_HPG))).reshape(d, NSA_GROUPS * LANES).astype(BF16)

    qkv = _matmul(h2d, w_qkv, BF16).reshape(bsz, seq, n_qkv)
    z = _matmul(h2d, w_z, BF16).reshape(bsz, seq, inner)
    gl = _matmul(h2d, w_gl, F32, tn=NSA_GROUPS * LANES).reshape(bsz, seq, NSA_GROUPS * LANES)
    kv_cmp = _nsa_compress(qkv, cmp_pe, cmp_w1, cmp_w2)
    y = _nsa_attention(qkv, kv_cmp, gl, z)
    return _matmul(y.reshape(bsz * seq, inner), w_out.astype(BF16), F32)


def _rg_kernel(xb_ref, z_ref, cw_ref, cb_ref, gw_ref, gb_ref, lam_ref, y_ref, xpad_ref, h_ref, *, ts):
    si = pl.program_id(2)
    halo = SUBLANES

    @pl.when(si == 0)
    def _():
        xpad_ref[0:halo, :] = jnp.zeros((halo, xpad_ref.shape[1]), F32)
        h_ref[...] = jnp.zeros_like(h_ref)

    x = xb_ref[...]
    xpad_ref[halo:halo + ts, :] = x
    xc = cb_ref[...] + cw_ref[RG_CONV - 1:RG_CONV, :] * x
    for k in range(RG_CONV - 1):
        shift = RG_CONV - 1 - k
        xc = xc + cw_ref[k:k + 1, :] * xpad_ref[halo - shift:halo - shift + ts, :]
    xpad_ref[0:halo, :] = x[ts - halo:ts, :]

    xcb = xc.astype(BF16)
    gate_i = _sigmoid(jnp.dot(xcb, gw_ref[0], preferred_element_type=F32) + gb_ref[0])
    gate_r = _sigmoid(jnp.dot(xcb, gw_ref[1], preferred_element_type=F32) + gb_ref[1])
    nl = -lam_ref[...]
    softplus = jnp.maximum(nl, 0.0) + jnp.log(1.0 + jnp.exp(-jnp.abs(nl)))
    log_a = (-RG_C) * gate_r * softplus
    a = jnp.exp(log_a)
    mult = jnp.sqrt(1.0 - jnp.exp(2.0 * log_a))
    row = lax.broadcasted_iota(jnp.int32, (ts, 1), 0)
    mult = jnp.where((row + si * ts) == 0, 1.0, mult)
    u = mult * gate_i * xc

    d = 1
    while d < ts:
        keep = row >= d
        a_sh = pltpu.roll(a, d, 0)
        u_sh = pltpu.roll(u, d, 0)
        u = jnp.where(keep, a * u_sh + u, u)
        a = jnp.where(keep, a * a_sh, a)
        d *= 2
    hs = a * h_ref[...] + u
    h_ref[...] = hs[ts - 1:ts, :]
    y_ref[...] = (hs * _silu(z_ref[...].astype(F32))).astype(y_ref.dtype)


def _rg_core(xb, z, conv_w, conv_b, gate_w, gate_b, lam, ts=512):
    bsz, s, w = xb.shape
    cb = w // RG_BLOCKS
    return pl.pallas_call(
        functools.partial(_rg_kernel, ts=ts),
        out_shape=jax.ShapeDtypeStruct((bsz, s, w), BF16),
        grid=(bsz, RG_BLOCKS, s // ts),
        in_specs=[
            pl.BlockSpec((None, ts, cb), lambda b, n, i: (b, i, n)),
            pl.BlockSpec((None, ts, cb), lambda b, n, i: (b, i, n)),
            pl.BlockSpec((RG_CONV, cb), lambda b, n, i: (0, n)),
            pl.BlockSpec((1, cb), lambda b, n, i: (0, n)),
            pl.BlockSpec((2, None, cb, cb), lambda b, n, i: (0, n, 0, 0)),
            pl.BlockSpec((2, None, 1, cb), lambda b, n, i: (0, n, 0, 0)),
            pl.BlockSpec((1, cb), lambda b, n, i: (0, n)),
        ],
        out_specs=pl.BlockSpec((None, ts, cb), lambda b, n, i: (b, i, n)),
        scratch_shapes=[pltpu.VMEM((ts + SUBLANES, cb), F32), pltpu.VMEM((1, cb), F32)],
        compiler_params=_cparams(("parallel", "parallel", "arbitrary")),
        name="rglru_core",
    )(xb, z, conv_w.astype(F32), conv_b.reshape(1, w).astype(F32), gate_w.astype(BF16),
      gate_b.reshape(2, RG_BLOCKS, 1, cb).astype(F32), lam.reshape(1, w).astype(F32))


def _rglru_mixer(h2d, bsz, seq, w_in, conv_w, conv_b, gate_w, gate_b, lam, w_out):
    width = w_out.shape[0]
    xb = _matmul(h2d, w_in[:, :width].astype(BF16), F32).reshape(bsz, seq, width)
    z = _matmul(h2d, w_in[:, width:].astype(BF16), BF16).reshape(bsz, seq, width)
    y = _rg_core(xb, z, conv_w, conv_b, gate_w, gate_b, lam)
    return _matmul(y.reshape(bsz * seq, width), w_out.astype(BF16), F32)


def _hg_kernel(q_ref, f_ref, v_ref, g_ref, lbl_ref, ng_ref, y_ref, state_ref, b_ref, k_ref, *, tc, layer):
    ci = pl.program_id(2)
    dk = q_ref.shape[1]
    nt = (((1,), (1,)), ((), ()))
    tn = (((0,), (0,)), ((), ()))

    @pl.when(ci == 0)
    def _():
        state_ref[...] = jnp.zeros_like(state_ref)

    lg = lbl_ref[...]
    e = jnp.exp(lg - jnp.max(lg, axis=0, keepdims=True))
    pl_sm = e * (1.0 / jnp.sum(e, axis=0, keepdims=True))
    lb = jnp.zeros((1, dk), F32)
    for r in range(1, layer + 1):
        lb = lb + pl_sm[r:r + 1, :]

    q = _silu(q_ref[...])
    fg = lb + (1.0 - lb) * _sigmoid(f_ref[...])
    kk = 1.0 - fg
    b = jnp.log(fg)
    row = lax.broadcasted_iota(jnp.int32, (tc, 1), 0)
    rc = row & (HG_CHUNK - 1)
    d = 1
    while d < HG_CHUNK:
        b = b + jnp.where(rc >= d, pltpu.roll(b, d, 0), 0.0)
        d *= 2
    b_ref[...] = b
    k_ref[...] = kk

    nsub = tc // HG_SUB
    lane = lax.broadcasted_iota(jnp.int32, (dk, LANES), 1)
    acc = jnp.zeros((tc, LANES), F32)
    for s in range(HG_SUB):
        b_s = jnp.concatenate(
            [jnp.broadcast_to(b_ref[i * HG_SUB + s:i * HG_SUB + s + 1, :], (HG_SUB, dk)) for i in range(nsub)], axis=0)
        k_s = jnp.concatenate(
            [jnp.broadcast_to(k_ref[i * HG_SUB + s:i * HG_SUB + s + 1, :], (HG_SUB, dk)) for i in range(nsub)], axis=0)
        m_s = q * jnp.exp(jnp.minimum(b - b_s, 0.0)) * k_s
        w_s = jnp.where(((lane & (HG_SUB - 1)) == s) & (lane < HG_CHUNK), 1.0, 0.0).astype(BF16)
        acc = acc + jnp.dot(m_s.astype(BF16), w_s, preferred_element_type=F32)

    col = lax.broadcasted_iota(jnp.int32, (HG_CHUNK, HG_CHUNK), 1)
    rw = lax.broadcasted_iota(jnp.int32, (HG_CHUNK, HG_CHUNK), 0)
    col_sub = lax.shift_right_logical(col, HG_SUB_SHIFT)
    rw_sub = lax.shift_right_logical(rw, HG_SUB_SHIFT)
    diag_mask = (col_sub == rw_sub) & (col <= rw)

    state = state_ref[...]
    nsc = HG_CHUNK // HG_SUB
    for c in range(tc // HG_CHUNK):
        r0 = c * HG_CHUNK
        bc = b[r0:r0 + HG_CHUNK, :]
        qc = q[r0:r0 + HG_CHUNK, :]
        kc = kk[r0:r0 + HG_CHUNK, :]
        vc = v_ref[r0:r0 + HG_CHUNK, :]
        o = lax.dot_general((qc * jnp.exp(bc)).astype(BF16), state.astype(BF16), nt, preferred_element_type=F32)
        blocks = [jnp.zeros((HG_SUB, HG_CHUNK), F32)]
        for i in range(1, nsc):
            r_i = bc[i * HG_SUB - 1:i * HG_SUB, :]
            q_i = (qc[i * HG_SUB:(i + 1) * HG_SUB, :] * jnp.exp(bc[i * HG_SUB:(i + 1) * HG_SUB, :] - r_i)).astype(BF16)
            k_i = (kc * jnp.exp(jnp.minimum(r_i - bc, 0.0))).astype(BF16)
            blocks.append(lax.dot_general(q_i, k_i, nt, preferred_element_type=F32))
        att_off = jnp.concatenate(blocks, axis=0)
        att = jnp.where(diag_mask, acc[r0:r0 + HG_CHUNK, 0:HG_CHUNK],
                        jnp.where(col_sub < rw_sub, att_off, 0.0))
        o = o + jnp.dot(att.astype(BF16), vc, preferred_element_type=F32)
        b_last = bc[HG_CHUNK - 1:HG_CHUNK, :]
        k_dec = (kc * jnp.exp(b_last - bc)).astype(BF16)
        state = state * jnp.exp(b_last) + lax.dot_general(vc, k_dec, tn, preferred_element_type=F32)
        ms = jnp.mean(o * o, axis=-1, keepdims=True)
        on = o * lax.rsqrt(ms + NORM_EPS) * ng_ref[...]
        gc = g_ref[r0:r0 + HG_CHUNK, :].astype(F32)
        y_ref[r0:r0 + HG_CHUNK, :] = (on * _silu(gc)).astype(y_ref.dtype)
    state_ref[...] = state


def _hg_core(qf, vg, lb_logits, norm_gain, layer, tc=256):
    bsz, s, _ = qf.shape
    dk = qf.shape[2] // (2 * HG_HEADS)
    dv = vg.shape[2] // (2 * HG_HEADS)
    nl = lb_logits.shape[0]
    return pl.pallas_call(
        functools.partial(_hg_kernel, tc=tc, layer=layer),
        out_shape=jax.ShapeDtypeStruct((bsz, s, HG_HEADS * dv), BF16),
        grid=(bsz, HG_HEADS, s // tc),
        in_specs=[
            pl.BlockSpec((None, tc, dk), lambda b, h, i: (b, i, h)),
            pl.BlockSpec((None, tc, dk), lambda b, h, i: (b, i, HG_HEADS + h)),
            pl.BlockSpec((None, tc, dv), lambda b, h, i: (b, i, h)),
            pl.BlockSpec((None, tc, dv), lambda b, h, i: (b, i, HG_HEADS + h)),
            pl.BlockSpec((nl, dk), lambda b, h, i: (0, h)),
            pl.BlockSpec((1, dv), lambda b, h, i: (0, 0)),
        ],
        out_specs=pl.BlockSpec((None, tc, dv), lambda b, h, i: (b, i, h)),
        scratch_shapes=[pltpu.VMEM((dv, dk), F32), pltpu.VMEM((tc, dk), F32), pltpu.VMEM((tc, dk), F32)],
        compiler_params=_cparams(("parallel", "parallel", "arbitrary")),
        name="hgrn2_core",
    )(qf, qf, vg, vg, lb_logits.astype(F32), norm_gain.reshape(1, dv).astype(F32))


def _hgrn2_mixer(h2d, bsz, seq, w_in, lb_logits, layer, norm_gain, w_out):
    val = w_out.shape[0]
    key = (w_in.shape[1] - 2 * val) // 2
    w_qf = w_in[:, :2 * key].astype(BF16)
    w_vg = w_in[:, 2 * key:].astype(BF16)
    qf = _matmul(h2d, w_qf, F32).reshape(bsz, seq, 2 * key)
    vg = _matmul(h2d, w_vg, BF16).reshape(bsz, seq, 2 * val)
    y = _hg_core(qf, vg, lb_logits, norm_gain, layer)
    return _matmul(y.reshape(bsz * seq, val), w_out.astype(BF16), F32)


def kernel(x, pre_norm_gain, post_norm_gain, nsa_w_in, nsa_cmp_pe, nsa_cmp_w1, nsa_cmp_w2, nsa_w_out,
           rg_w_in, rg_conv_w, rg_conv_b, rg_gate_w, rg_gate_b, rg_lambda, rg_w_out,
           hg_w_in, hg_lb_logits, hg_norm_gain, hg_w_out):
    bsz, seq, d = x.shape
    depth = pre_norm_gain.shape[0]
    x2d = x.reshape(bsz * seq, d)
    for i in range(depth):
        h = _prenorm(x2d, pre_norm_gain[i])
        kind, j = i % 3, i // 3
        if kind == 0:
            y = _nsa_mixer(h, bsz, seq, nsa_w_in[j], nsa_cmp_pe[j], nsa_cmp_w1[j], nsa_cmp_w2[j], nsa_w_out[j])
        elif kind == 1:
            y = _rglru_mixer(h, bsz, seq, rg_w_in[j], rg_conv_w[j], rg_conv_b[j], rg_gate_w[j], rg_gate_b[j],
                             rg_lambda[j], rg_w_out[j])
        else:
            y = _hgrn2_mixer(h, bsz, seq, hg_w_in[j], hg_lb_logits, i, hg_norm_gain[j], hg_w_out[j])
        x2d = _postnorm_residual(x2d, y, post_norm_gain[i])
    return x2d.reshape(bsz, seq, d)
```

```python
import functools
import math

import jax
import jax.numpy as jnp
from jax import lax
from jax.experimental import pallas as pl
from jax.experimental.pallas import tpu as pltpu

F32 = jnp.float32
BF16 = jnp.bfloat16

NORM_EPS = 1e-6
NEG_INF = -1e30
FORCE_SCORE = 1e6

LANES = 128
SUBLANES = 8
V7X_VMEM_LIMIT_BYTES = 56 * 1024 * 1024

NSA_HEADS = 32
NSA_GROUPS = 4
NSA_HPG = NSA_HEADS // NSA_GROUPS
HEAD_DIM = 128
CMP_LEN = 32
CMP_STRIDE = 16
SEL_LEN = 64
SEL_SHIFT = 6
SEL_TOPK = 16
WINDOW = 512
ALIBI_MAX_EXP = 8.0
NSA_MASK_BIG = 2.0 ** 40

RG_BLOCKS = 16
RG_CONV = 4
RG_C = 8.0

HG_HEADS = 32
HG_CHUNK = 64
HG_SUB = 16
HG_SUB_SHIFT = 4


def _cparams(sem):
    return pltpu.CompilerParams(dimension_semantics=sem, vmem_limit_bytes=V7X_VMEM_LIMIT_BYTES)


def _sigmoid(x):
    return 0.5 * jnp.tanh(0.5 * x) + 0.5


def _silu(x):
    return x * _sigmoid(x)


def _prenorm_kernel(x_ref, g_ref, o_ref):
    x = x_ref[...]
    ms = jnp.mean(x * x, axis=-1, keepdims=True)
    o_ref[...] = (x * lax.rsqrt(ms + NORM_EPS) * g_ref[...]).astype(o_ref.dtype)


def _prenorm(x2d, gain, tm=256):
    t, d = x2d.shape
    return pl.pallas_call(
        _prenorm_kernel,
        out_shape=jax.ShapeDtypeStruct((t, d), BF16),
        grid=(t // tm,),
        in_specs=[pl.BlockSpec((tm, d), lambda i: (i, 0)), pl.BlockSpec((1, d), lambda i: (0, 0))],
        out_specs=pl.BlockSpec((tm, d), lambda i: (i, 0)),
        compiler_params=_cparams(("parallel",)),
        name="prenorm",
    )(x2d, gain.reshape(1, d).astype(F32))


def _postnorm_kernel(x_ref, y_ref, g_ref, o_ref):
    y = y_ref[...].astype(F32)
    ms = jnp.mean(y * y, axis=-1, keepdims=True)
    o_ref[...] = x_ref[...] + y * lax.rsqrt(ms + NORM_EPS) * g_ref[...]


def _postnorm_residual(x2d, y2d, gain, tm=256):
    t, d = x2d.shape
    return pl.pallas_call(
        _postnorm_kernel,
        out_shape=jax.ShapeDtypeStruct((t, d), F32),
        grid=(t // tm,),
        in_specs=[pl.BlockSpec((tm, d), lambda i: (i, 0)), pl.BlockSpec((tm, d), lambda i: (i, 0)),
                  pl.BlockSpec((1, d), lambda i: (0, 0))],
        out_specs=pl.BlockSpec((tm, d), lambda i: (i, 0)),
        compiler_params=_cparams(("parallel",)),
        name="postnorm_residual",
    )(x2d, y2d, gain.reshape(1, d).astype(F32))


def _proj_kernel(a_ref, w_ref, o_ref, wb_ref):
    @pl.when(pl.program_id(1) == 0)
    def _():
        wb_ref[...] = w_ref[...].astype(BF16)

    o_ref[...] = jnp.dot(a_ref[...], wb_ref[...], preferred_element_type=F32).astype(o_ref.dtype)


def _proj(a, w, out_dtype, col0=0, ncols=None, tm=1024, tn=512):
    m, kdim = a.shape
    ncols = w.shape[1] - col0 if ncols is None else ncols
    tm, tn = min(tm, m), min(tn, ncols)
    assert m % tm == 0 and ncols % tn == 0 and col0 % tn == 0 and w.shape[0] == kdim
    cb0 = col0 // tn
    return pl.pallas_call(
        _proj_kernel,
        out_shape=jax.ShapeDtypeStruct((m, ncols), out_dtype),
        grid=(ncols // tn, m // tm),
        in_specs=[pl.BlockSpec((tm, kdim), lambda j, i: (i, 0)), pl.BlockSpec((kdim, tn), lambda j, i: (0, cb0 + j))],
        out_specs=pl.BlockSpec((tm, tn), lambda j, i: (i, j)),
        scratch_shapes=[pltpu.VMEM((kdim, tn), BF16)],
        compiler_params=_cparams(("parallel", "arbitrary")),
        name="projection",
    )(a, w)


def _gelu_tanh(x):
    return 0.5 * x * (1.0 + jnp.tanh(math.sqrt(2.0 / math.pi) * (x + 0.044715 * (x * x * x))))


def _compress_kernel(x_ref, pe_ref, w1_ref, w2_ref, o_ref, xf_ref):
    s = x_ref.shape[0]
    nch = s // CMP_STRIDE
    xf_ref[...] = x_ref[...].astype(F32)
    acc_lo = jnp.zeros((nch, HEAD_DIM), F32)
    acc_hi = jnp.zeros((nch, HEAD_DIM), F32)
    for l in range(CMP_STRIDE):
        xl = xf_ref[pl.ds(l, nch, stride=CMP_STRIDE), :]
        lo_in = (xl + pe_ref[l:l + 1, :]).astype(BF16)
        hi_in = (xl + pe_ref[CMP_STRIDE + l:CMP_STRIDE + l + 1, :]).astype(BF16)
        acc_lo += jnp.dot(lo_in, w1_ref[l * HEAD_DIM:(l + 1) * HEAD_DIM, :], preferred_element_type=F32)
        acc_hi += jnp.dot(hi_in, w1_ref[(CMP_STRIDE + l) * HEAD_DIM:(CMP_STRIDE + l + 1) * HEAD_DIM, :],
                          preferred_element_type=F32)
    pre = acc_lo + pltpu.roll(acc_hi, nch - 1, 0)
    hid = _gelu_tanh(pre).astype(BF16)
    o_ref[...] = jnp.dot(hid, w2_ref[...], preferred_element_type=F32).astype(o_ref.dtype)


def _nsa_compress(qkv, pe, w1, w2):
    bsz, s, _ = qkv.shape
    nch = s // CMP_STRIDE
    kv_block0 = NSA_HEADS
    return pl.pallas_call(
        _compress_kernel,
        out_shape=jax.ShapeDtypeStruct((bsz, 2, NSA_GROUPS, nch, HEAD_DIM), BF16),
        grid=(bsz, 2, NSA_GROUPS),
        in_specs=[
            pl.BlockSpec((None, s, HEAD_DIM), lambda b, w, g: (b, 0, kv_block0 + w * NSA_GROUPS + g)),
            pl.BlockSpec((None, CMP_LEN, HEAD_DIM), lambda b, w, g: (w, 0, 0)),
            pl.BlockSpec((None, CMP_LEN * HEAD_DIM, HEAD_DIM), lambda b, w, g: (w, 0, 0)),
            pl.BlockSpec((None, HEAD_DIM, HEAD_DIM), lambda b, w, g: (w, 0, 0)),
        ],
        out_specs=pl.BlockSpec((None, None, None, nch, HEAD_DIM), lambda b, w, g: (b, w, g, 0, 0)),
        scratch_shapes=[pltpu.VMEM((s, HEAD_DIM), F32)],
        compiler_params=_cparams(("parallel", "parallel", "parallel")),
        name="nsa_compress",
    )(qkv, pe.astype(F32), w1.astype(BF16), w2.astype(BF16))


def _split3(x):
    hi = x.astype(BF16)
    r1 = x - hi.astype(F32)
    mid = r1.astype(BF16)
    lo = (r1 - mid.astype(F32)).astype(BF16)
    return hi, mid, lo


def _softmax_rows(s):
    m = jnp.max(s, axis=-1, keepdims=True)
    p = jnp.exp(s - m)
    return p, jnp.sum(p, axis=-1, keepdims=True)


def _nsa_attn_kernel(slopes_ref, slx_ref, kx_ref, q_ref, kcmp_ref, vcmp_ref, ks_ref, vs_ref, kw_ref, vw_ref,
                     gl_ref, z_ref, y_ref, ksx_ref, kwx_ref, qx_ref, oc_ref, *, tq, seq):
    qi = pl.program_id(2)
    t0 = qi * tq
    scale = HEAD_DIM ** -0.5
    n_cmp_pad = kcmp_ref.shape[0]
    n_sel = seq // SEL_LEN
    rows = NSA_HPG * tq
    nt = (((1,), (1,)), ((), ()))

    @pl.when(qi == 0)
    def _():
        kx = kx_ref[...]
        lane_k = lax.broadcasted_iota(jnp.int32, kx.shape, 1)
        ksx_ref[:, 0:HEAD_DIM] = ks_ref[...]
        ksx_ref[:, HEAD_DIM:] = kx
        kwx_ref[:, 0:HEAD_DIM] = kw_ref[...]
        kwx_ref[:, HEAD_DIM:] = jnp.where(lane_k >= n_sel, kx, jnp.zeros_like(kx))

    tpos_i = t0 + lax.broadcasted_iota(jnp.int32, (tq, 1), 0)
    tpos = tpos_i.astype(F32)

    cmp_end = (lax.broadcasted_iota(jnp.int32, (1, n_cmp_pad), 1) * CMP_STRIDE + (CMP_LEN - 1)).astype(F32)
    dist_c = tpos - cmp_end
    valid_c = dist_c >= 0.0
    row_valid = tpos >= float(CMP_LEN - 1)
    kcmp = kcmp_ref[...]
    vcmp = vcmp_ref[...]
    pg = jnp.zeros((tq, n_cmp_pad), F32)
    for j in range(NSA_HPG):
        qh = q_ref[:, j * HEAD_DIM:(j + 1) * HEAD_DIM]
        sl = slopes_ref[j:j + 1, :]
        s = lax.dot_general(qh, kcmp, nt, preferred_element_type=F32) * scale - sl * dist_c
        s = jnp.where(valid_c, s, NEG_INF)
        p, l = _softmax_rows(s)
        p = jnp.where(valid_c & row_valid, p * (1.0 / l), 0.0)
        pg = pg + p
        oc_ref[j * tq:(j + 1) * tq, :] = jnp.dot(p.astype(BF16), vcmp, preferred_element_type=F32)

    n_idx = lax.broadcasted_iota(jnp.int32, (n_cmp_pad, LANES), 0)
    j_idx = lax.broadcasted_iota(jnp.int32, (n_cmp_pad, LANES), 1)
    dd = n_idx - (SEL_LEN // CMP_STRIDE) * j_idx + (CMP_LEN // CMP_STRIDE - 1)
    pool = jnp.where((dd == 0) | (dd == 4), 1.0, jnp.where((dd >= 1) & (dd <= 3), 2.0, 0.0)).astype(BF16)
    p_slc = jnp.zeros((tq, LANES), F32)
    for part in _split3(pg):
        p_slc = p_slc + jnp.dot(part, pool, preferred_element_type=F32)

    blk = lax.broadcasted_iota(jnp.int32, (1, LANES), 1)
    cur = lax.shift_right_logical(tpos_i, SEL_SHIFT)
    forced = (blk == 0) | (blk == cur) | (blk == cur - 1)
    future = blk > cur
    score = jnp.where(forced, FORCE_SCORE, jnp.where(future, -1.0, p_slc))
    score = jnp.where(blk < n_sel, score, -2.0)
    rank = jnp.zeros((tq, LANES), F32)
    for jp in range(n_sel):
        col = score[:, jp:jp + 1]
        ahead = (col > score) | ((col == score) & (blk > jp))
        rank = rank + jnp.where(ahead, 1.0, 0.0)
    sel01 = jnp.where(rank < float(min(SEL_TOPK, n_sel)), 1.0, 0.0)
    penalty = jnp.where(blk < n_sel, (sel01 - 1.0) * NSA_MASK_BIG, 0.0)

    for j in range(NSA_HPG):
        qx_ref[j * tq:(j + 1) * tq, 0:HEAD_DIM] = q_ref[:, j * HEAD_DIM:(j + 1) * HEAD_DIM]
        qx_ref[j * tq:(j + 1) * tq, HEAD_DIM:] = (penalty + slx_ref[j:j + 1, :]).astype(BF16)

    t_loc = lax.broadcasted_iota(jnp.int32, (tq, 1), 0)
    k_loc = lax.broadcasted_iota(jnp.int32, (1, tq), 1)
    causal = k_loc <= t_loc
    far_ok = k_loc > t_loc
    n_back = WINDOW // tq
    exp2_scale = scale * math.log2(math.e)
    gates = _sigmoid(gl_ref[...].astype(F32))
    lane_g = lax.broadcasted_iota(jnp.int32, (1, LANES), 1)

    def attend(qx, pieces):
        ss = []
        for kx_p, _, mask in pieces:
            s = lax.dot_general(qx, kx_p, nt, preferred_element_type=F32)
            ss.append(s if mask is None else jnp.where(mask, s, NEG_INF))
        m = functools.reduce(jnp.maximum, [jnp.max(s, axis=-1, keepdims=True) for s in ss])
        l = 0.0
        o = 0.0
        for s, (_, v_p, _) in zip(ss, pieces):
            p = jnp.exp2((s - m) * exp2_scale)
            l = l + jnp.sum(p, axis=-1, keepdims=True)
            o = o + jnp.dot(p.astype(BF16), v_p, preferred_element_type=F32)
        return o * (1.0 / l)

    def variant(n):
        d0 = (n - 1) * tq

        def head_body(j, carry):
            r0 = pl.multiple_of(j * tq, tq)
            c0 = pl.multiple_of(j * HEAD_DIM, HEAD_DIM)
            qx = qx_ref[pl.ds(r0, tq), :]
            sel_pieces = [(ksx_ref[d0:d0 + tq, :], vs_ref[d0:d0 + tq, :], causal)]
            if n > 1:
                sel_pieces.append((ksx_ref[0:d0, :], vs_ref[0:d0, :], None))
            o_slc = attend(qx, sel_pieces)
            win_pieces = [(kwx_ref[d0:d0 + tq, :], vw_ref[d0:d0 + tq, :], causal)]
            for w in range(1, min(n - 1, n_back) + 1):
                k0 = d0 - w * tq
                win_pieces.append((kwx_ref[k0:k0 + tq, :], vw_ref[k0:k0 + tq, :], far_ok if w == n_back else None))
            o_win = attend(qx, win_pieces)

            def gate(branch):
                pick = lane_g == branch * NSA_HPG + j
                return jnp.sum(jnp.where(pick, gates, 0.0), axis=-1, keepdims=True)

            o = gate(0) * oc_ref[pl.ds(r0, tq), :] + gate(1) * o_slc + gate(2) * o_win
            zj = z_ref[:, pl.ds(c0, HEAD_DIM)].astype(F32)
            y_ref[:, pl.ds(c0, HEAD_DIM)] = (o * _silu(zj)).astype(y_ref.dtype)
            return carry

        lax.fori_loop(0, NSA_HPG, head_body, 0)

    for n in range(1, seq // tq + 1):
        pl.when(qi == n - 1)(functools.partial(variant, n))


def _alibi_slopes():
    return 2.0 ** (-ALIBI_MAX_EXP * jnp.arange(1, NSA_HEADS + 1, dtype=F32) / NSA_HEADS)


def _nsa_tables(seq):
    slopes = _alibi_slopes()
    slope_rows = jnp.broadcast_to(slopes.reshape(NSA_GROUPS, NSA_HPG, 1), (NSA_GROUPS, NSA_HPG, LANES))
    n_sel = seq // SEL_LEN
    parts = []
    rest = slopes * (HEAD_DIM ** 0.5)
    for _ in range(3):
        part = rest.astype(BF16).astype(F32)
        parts.append(part)
        rest = rest - part
    slx = jnp.zeros((NSA_HEADS, LANES), F32)
    for c, part in enumerate(parts + parts):
        slx = slx.at[:, n_sel + c].set(part)
    slx = slx.reshape(NSA_GROUPS, NSA_HPG, LANES)
    key = jnp.arange(seq, dtype=jnp.int32)
    lane = jnp.arange(LANES, dtype=jnp.int32)[None, :]
    hi = ((key // SEL_LEN) * SEL_LEN).astype(F32)[:, None]
    lo = (key % SEL_LEN).astype(F32)[:, None]
    kx = jnp.where(lane == (key // SEL_LEN)[:, None], 1.0, 0.0)
    kx = jnp.where((lane >= n_sel) & (lane < n_sel + 3), hi, kx)
    kx = jnp.where((lane >= n_sel + 3) & (lane < n_sel + 6), lo, kx)
    return slope_rows, slx, kx.astype(BF16)


def _nsa_attention(qkv, kv_cmp, gl, z, tq=256):
    bsz, s, _ = qkv.shape
    gw = NSA_HPG * HEAD_DIM
    ncp = kv_cmp.shape[3]
    kvb = NSA_HEADS
    rows = NSA_HPG * tq
    assert WINDOW % tq == 0 and s // SEL_LEN + 6 <= LANES and tq % SEL_LEN == 0
    slope_rows, slx, kx = _nsa_tables(s)

    def kv_spec(which):
        return pl.BlockSpec((None, s, HEAD_DIM), lambda b, g, i: (b, 0, kvb + which * NSA_GROUPS + g))

    return pl.pallas_call(
        functools.partial(_nsa_attn_kernel, tq=tq, seq=s),
        out_shape=jax.ShapeDtypeStruct((bsz, s, NSA_HEADS * HEAD_DIM), BF16),
        grid=(bsz, NSA_GROUPS, s // tq),
        in_specs=[
            pl.BlockSpec((None, NSA_HPG, LANES), lambda b, g, i: (g, 0, 0)),
            pl.BlockSpec((None, NSA_HPG, LANES), lambda b, g, i: (g, 0, 0)),
            pl.BlockSpec((s, LANES), lambda b, g, i: (0, 0)),
            pl.BlockSpec((None, tq, gw), lambda b, g, i: (b, i, g)),
            pl.BlockSpec((None, None, None, ncp, HEAD_DIM), lambda b, g, i: (b, 0, g, 0, 0)),
            pl.BlockSpec((None, None, None, ncp, HEAD_DIM), lambda b, g, i: (b, 1, g, 0, 0)),
            kv_spec(2), kv_spec(3), kv_spec(4), kv_spec(5),
            pl.BlockSpec((None, tq, LANES), lambda b, g, i: (b, i, g)),
            pl.BlockSpec((None, tq, gw), lambda b, g, i: (b, i, g)),
        ],
        out_specs=pl.BlockSpec((None, tq, gw), lambda b, g, i: (b, i, g)),
        scratch_shapes=[
            pltpu.VMEM((s, 2 * HEAD_DIM), BF16),
            pltpu.VMEM((s, 2 * HEAD_DIM), BF16),
            pltpu.VMEM((rows, 2 * HEAD_DIM), BF16),
            pltpu.VMEM((rows, HEAD_DIM), F32),
        ],
        compiler_params=_cparams(("parallel", "parallel", "arbitrary")),
        name="nsa_attention",
    )(slope_rows, slx, kx, qkv, kv_cmp, kv_cmp, qkv, qkv, qkv, qkv, gl, z)


def _nsa_mixer(h2d, bsz, seq, w_in, cmp_pe, cmp_w1, cmp_w2, w_out):
    d = h2d.shape[1]
    inner = NSA_HEADS * HEAD_DIM
    kv = NSA_GROUPS * HEAD_DIM
    n_qkv = inner + 6 * kv
    n_gl = 3 * NSA_HEADS
    w_z = w_in[:, n_qkv + n_gl:]
    w_gl = w_in[:, n_qkv:n_qkv + n_gl].reshape(d, 3, NSA_GROUPS, NSA_HPG).transpose(0, 2, 1, 3)
    w_gl = w_gl.reshape(d, NSA_GROUPS, 3 * NSA_HPG)
    w_gl = jnp.pad(w_gl, ((0, 0), (0, 0), (0, LANES - 3 * NSA_HPG))).reshape(d, NSA_GROUPS * LANES)

    qkv = _proj(h2d, w_in, BF16, 0, n_qkv).reshape(bsz, seq, n_qkv)
    z = _proj(h2d, w_z, BF16).reshape(bsz, seq, inner)
    gl = _proj(h2d, w_gl, F32).reshape(bsz, seq, NSA_GROUPS * LANES)
    kv_cmp = _nsa_compress(qkv, cmp_pe, cmp_w1, cmp_w2)
    y = _nsa_attention(qkv, kv_cmp, gl, z)
    return _proj(y.reshape(bsz * seq, inner), w_out, F32)


def _rg_kernel(xb_ref, z_ref, cw_ref, cb_ref, gw_ref, gb_ref, lam_ref, y_ref, xpad_ref, h_ref, *, ts):
    si = pl.program_id(2)
    halo = SUBLANES

    @pl.when(si == 0)
    def _():
        xpad_ref[0:halo, :] = jnp.zeros((halo, xpad_ref.shape[1]), F32)
        h_ref[...] = jnp.zeros_like(h_ref)

    x = xb_ref[...]
    xpad_ref[halo:halo + ts, :] = x
    xc = cb_ref[...] + cw_ref[RG_CONV - 1:RG_CONV, :] * x
    for k in range(RG_CONV - 1):
        shift = RG_CONV - 1 - k
        xc = xc + cw_ref[k:k + 1, :] * xpad_ref[halo - shift:halo - shift + ts, :]
    xpad_ref[0:halo, :] = x[ts - halo:ts, :]

    xcb = xc.astype(BF16)
    gate_i = _sigmoid(jnp.dot(xcb, gw_ref[0], preferred_element_type=F32) + gb_ref[0])
    gate_r = _sigmoid(jnp.dot(xcb, gw_ref[1], preferred_element_type=F32) + gb_ref[1])
    nl = -lam_ref[...]
    softplus = jnp.maximum(nl, 0.0) + jnp.log(1.0 + jnp.exp(-jnp.abs(nl)))
    log_a = (-RG_C) * gate_r * softplus
    a = jnp.exp(log_a)
    mult = jnp.sqrt(1.0 - jnp.exp(2.0 * log_a))
    row = lax.broadcasted_iota(jnp.int32, (ts, 1), 0)
    mult = jnp.where((row + si * ts) == 0, 1.0, mult)
    u = mult * gate_i * xc

    d = 1
    while d < ts:
        keep = row >= d
        a_sh = pltpu.roll(a, d, 0)
        u_sh = pltpu.roll(u, d, 0)
        u = jnp.where(keep, a * u_sh + u, u)
        a = jnp.where(keep, a * a_sh, a)
        d *= 2
    hs = a * h_ref[...] + u
    h_ref[...] = hs[ts - 1:ts, :]
    y_ref[...] = (hs * _silu(z_ref[...].astype(F32))).astype(y_ref.dtype)


def _rg_core(xb, z, conv_w, conv_b, gate_w, gate_b, lam, ts=512):
    bsz, s, w = xb.shape
    cb = w // RG_BLOCKS
    return pl.pallas_call(
        functools.partial(_rg_kernel, ts=ts),
        out_shape=jax.ShapeDtypeStruct((bsz, s, w), BF16),
        grid=(bsz, RG_BLOCKS, s // ts),
        in_specs=[
            pl.BlockSpec((None, ts, cb), lambda b, n, i: (b, i, n)),
            pl.BlockSpec((None, ts, cb), lambda b, n, i: (b, i, n)),
            pl.BlockSpec((RG_CONV, cb), lambda b, n, i: (0, n)),
            pl.BlockSpec((1, cb), lambda b, n, i: (0, n)),
            pl.BlockSpec((2, None, cb, cb), lambda b, n, i: (0, n, 0, 0)),
            pl.BlockSpec((2, None, 1, cb), lambda b, n, i: (0, n, 0, 0)),
            pl.BlockSpec((1, cb), lambda b, n, i: (0, n)),
        ],
        out_specs=pl.BlockSpec((None, ts, cb), lambda b, n, i: (b, i, n)),
        scratch_shapes=[pltpu.VMEM((ts + SUBLANES, cb), F32), pltpu.VMEM((1, cb), F32)],
        compiler_params=_cparams(("parallel", "parallel", "arbitrary")),
        name="rglru_core",
    )(xb, z, conv_w.astype(F32), conv_b.reshape(1, w).astype(F32), gate_w.astype(BF16),
      gate_b.reshape(2, RG_BLOCKS, 1, cb).astype(F32), lam.reshape(1, w).astype(F32))


def _rglru_mixer(h2d, bsz, seq, w_in, conv_w, conv_b, gate_w, gate_b, lam, w_out):
    width = w_out.shape[0]
    xb = _proj(h2d, w_in, F32, 0, width).reshape(bsz, seq, width)
    z = _proj(h2d, w_in, BF16, width, width).reshape(bsz, seq, width)
    y = _rg_core(xb, z, conv_w, conv_b, gate_w, gate_b, lam)
    return _proj(y.reshape(bsz * seq, width), w_out, F32)


def _hg_kernel(q_ref, f_ref, v_ref, g_ref, lbl_ref, ng_ref, y_ref, state_ref, b_ref, k_ref, *, tc, layer):
    ci = pl.program_id(2)
    dk = q_ref.shape[1]
    nt = (((1,), (1,)), ((), ()))
    tn = (((0,), (0,)), ((), ()))

    @pl.when(ci == 0)
    def _():
        state_ref[...] = jnp.zeros_like(state_ref)

    lg = lbl_ref[...]
    e = jnp.exp(lg - jnp.max(lg, axis=0, keepdims=True))
    pl_sm = e * (1.0 / jnp.sum(e, axis=0, keepdims=True))
    lb = jnp.zeros((1, dk), F32)
    for r in range(1, layer + 1):
        lb = lb + pl_sm[r:r + 1, :]

    q = _silu(q_ref[...])
    fg = lb + (1.0 - lb) * _sigmoid(f_ref[...])
    kk = 1.0 - fg
    b = jnp.log(fg)
    row = lax.broadcasted_iota(jnp.int32, (tc, 1), 0)
    rc = row & (HG_CHUNK - 1)
    d = 1
    while d < HG_CHUNK:
        b = b + jnp.where(rc >= d, pltpu.roll(b, d, 0), 0.0)
        d *= 2
    b_ref[...] = b
    k_ref[...] = kk

    nsub = tc // HG_SUB
    lane = lax.broadcasted_iota(jnp.int32, (dk, LANES), 1)
    acc = jnp.zeros((tc, LANES), F32)
    for s in range(HG_SUB):
        b_s = jnp.concatenate(
            [jnp.broadcast_to(b_ref[i * HG_SUB + s:i * HG_SUB + s + 1, :], (HG_SUB, dk)) for i in range(nsub)], axis=0)
        k_s = jnp.concatenate(
            [jnp.broadcast_to(k_ref[i * HG_SUB + s:i * HG_SUB + s + 1, :], (HG_SUB, dk)) for i in range(nsub)], axis=0)
        m_s = q * jnp.exp(jnp.minimum(b - b_s, 0.0)) * k_s
        w_s = jnp.where(((lane & (HG_SUB - 1)) == s) & (lane < HG_CHUNK), 1.0, 0.0).astype(BF16)
        acc = acc + jnp.dot(m_s.astype(BF16), w_s, preferred_element_type=F32)

    col = lax.broadcasted_iota(jnp.int32, (HG_CHUNK, HG_CHUNK), 1)
    rw = lax.broadcasted_iota(jnp.int32, (HG_CHUNK, HG_CHUNK), 0)
    col_sub = lax.shift_right_logical(col, HG_SUB_SHIFT)
    rw_sub = lax.shift_right_logical(rw, HG_SUB_SHIFT)
    diag_mask = (col_sub == rw_sub) & (col <= rw)

    state = state_ref[...]
    nsc = HG_CHUNK // HG_SUB
    for c in range(tc // HG_CHUNK):
        r0 = c * HG_CHUNK
        bc = b[r0:r0 + HG_CHUNK, :]
        qc = q[r0:r0 + HG_CHUNK, :]
        kc = kk[r0:r0 + HG_CHUNK, :]
        vc = v_ref[r0:r0 + HG_CHUNK, :]
        o = lax.dot_general((qc * jnp.exp(bc)).astype(BF16), state.astype(BF16), nt, preferred_element_type=F32)
        blocks = [jnp.zeros((HG_SUB, HG_CHUNK), F32)]
        for i in range(1, nsc):
            r_i = bc[i * HG_SUB - 1:i * HG_SUB, :]
            q_i = (qc[i * HG_SUB:(i + 1) * HG_SUB, :] * jnp.exp(bc[i * HG_SUB:(i + 1) * HG_SUB, :] - r_i)).astype(BF16)
            k_i = (kc * jnp.exp(jnp.minimum(r_i - bc, 0.0))).astype(BF16)
            blocks.append(lax.dot_general(q_i, k_i, nt, preferred_element_type=F32))
        att_off = jnp.concatenate(blocks, axis=0)
        att = jnp.where(diag_mask, acc[r0:r0 + HG_CHUNK, 0:HG_CHUNK],
                        jnp.where(col_sub < rw_sub, att_off, 0.0))
        o = o + jnp.dot(att.astype(BF16), vc, preferred_element_type=F32)
        b_last = bc[HG_CHUNK - 1:HG_CHUNK, :]
        k_dec = (kc * jnp.exp(b_last - bc)).astype(BF16)
        state = state * jnp.exp(b_last) + lax.dot_general(vc, k_dec, tn, preferred_element_type=F32)
        ms = jnp.mean(o * o, axis=-1, keepdims=True)
        on = o * lax.rsqrt(ms + NORM_EPS) * ng_ref[...]
        gc = g_ref[r0:r0 + HG_CHUNK, :].astype(F32)
        y_ref[r0:r0 + HG_CHUNK, :] = (on * _silu(gc)).astype(y_ref.dtype)
    state_ref[...] = state


def _hg_core(qf, vg, lb_logits, norm_gain, layer, tc=256):
    bsz, s, _ = qf.shape
    dk = qf.shape[2] // (2 * HG_HEADS)
    dv = vg.shape[2] // (2 * HG_HEADS)
    nl = lb_logits.shape[0]
    return pl.pallas_call(
        functools.partial(_hg_kernel, tc=tc, layer=layer),
        out_shape=jax.ShapeDtypeStruct((bsz, s, HG_HEADS * dv), BF16),
        grid=(bsz, HG_HEADS, s // tc),
        in_specs=[
            pl.BlockSpec((None, tc, dk), lambda b, h, i: (b, i, h)),
            pl.BlockSpec((None, tc, dk), lambda b, h, i: (b, i, HG_HEADS + h)),
            pl.BlockSpec((None, tc, dv), lambda b, h, i: (b, i, h)),
            pl.BlockSpec((None, tc, dv), lambda b, h, i: (b, i, HG_HEADS + h)),
            pl.BlockSpec((nl, dk), lambda b, h, i: (0, h)),
            pl.BlockSpec((1, dv), lambda b, h, i: (0, 0)),
        ],
        out_specs=pl.BlockSpec((None, tc, dv), lambda b, h, i: (b, i, h)),
        scratch_shapes=[pltpu.VMEM((dv, dk), F32), pltpu.VMEM((tc, dk), F32), pltpu.VMEM((tc, dk), F32)],
        compiler_params=_cparams(("parallel", "parallel", "arbitrary")),
        name="hgrn2_core",
    )(qf, qf, vg, vg, lb_logits.astype(F32), norm_gain.reshape(1, dv).astype(F32))


def _hgrn2_mixer(h2d, bsz, seq, w_in, lb_logits, layer, norm_gain, w_out):
    val = w_out.shape[0]
    key = (w_in.shape[1] - 2 * val) // 2
    qf = _proj(h2d, w_in, F32, 0, 2 * key).reshape(bsz, seq, 2 * key)
    vg = _proj(h2d, w_in, BF16, 2 * key, 2 * val).reshape(bsz, seq, 2 * val)
    y = _hg_core(qf, vg, lb_logits, norm_gain, layer)
    return _proj(y.reshape(bsz * seq, val), w_out, F32)


def kernel(x, pre_norm_gain, post_norm_gain, nsa_w_in, nsa_cmp_pe, nsa_cmp_w1, nsa_cmp_w2, nsa_w_out,
           rg_w_in, rg_conv_w, rg_conv_b, rg_gate_w, rg_gate_b, rg_lambda, rg_w_out,
           hg_w_in, hg_lb_logits, hg_norm_gain, hg_w_out):
    bsz, seq, d = x.shape
    depth = pre_norm_gain.shape[0]
    x2d = x.reshape(bsz * seq, d)
    for i in range(depth):
        h = _prenorm(x2d, pre_norm_gain[i])
        kind, j = i % 3, i // 3
        if kind == 0:
            y = _nsa_mixer(h, bsz, seq, nsa_w_in[j], nsa_cmp_pe[j], nsa_cmp_w1[j], nsa_cmp_w2[j], nsa_w_out[j])
        elif kind == 1:
            y = _rglru_mixer(h, bsz, seq, rg_w_in[j], rg_conv_w[j], rg_conv_b[j], rg_gate_w[j], rg_gate_b[j],
                             rg_lambda[j], rg_w_out[j])
        else:
            y = _hgrn2_mixer(h, bsz, seq, hg_w_in[j], hg_lb_logits, i, hg_norm_gain[j], hg_w_out[j])
        x2d = _postnorm_residual(x2d, y, post_norm_gain[i])
    return x2d.reshape(bsz, seq, d)
```

```python
import functools
import math

import jax
import jax.numpy as jnp
from jax import lax
from jax.experimental import pallas as pl
from jax.experimental.pallas import tpu as pltpu

F32 = jnp.float32
BF16 = jnp.bfloat16

NORM_EPS = 1e-6
NEG_INF = -1e30
FORCE_SCORE = 1e6

LANES = 128
SUBLANES = 8
V7X_VMEM_LIMIT_BYTES = 56 * 1024 * 1024

NSA_HEADS = 32
NSA_GROUPS = 4
NSA_HPG = NSA_HEADS // NSA_GROUPS
HEAD_DIM = 128
CMP_LEN = 32
CMP_STRIDE = 16
SEL_LEN = 64
SEL_SHIFT = 6
SEL_TOPK = 16
WINDOW = 512
ALIBI_MAX_EXP = 8.0
NSA_MASK_BIG = 2.0 ** 40

RG_BLOCKS = 16
RG_CONV = 4
RG_C = 8.0

HG_HEADS = 32
HG_CHUNK = 64
HG_SUB = 16
HG_SUB_SHIFT = 4


def _cparams(sem):
    return pltpu.CompilerParams(dimension_semantics=sem, vmem_limit_bytes=V7X_VMEM_LIMIT_BYTES)


def _sigmoid(x):
    return 0.5 * jnp.tanh(0.5 * x) + 0.5


def _silu(x):
    return x * _sigmoid(x)


def _prenorm_kernel(x_ref, g_ref, o_ref):
    x = x_ref[...]
    ms = jnp.mean(x * x, axis=-1, keepdims=True)
    o_ref[...] = (x * lax.rsqrt(ms + NORM_EPS) * g_ref[...]).astype(o_ref.dtype)


def _prenorm(x2d, gain, tm=256):
    t, d = x2d.shape
    return pl.pallas_call(
        _prenorm_kernel,
        out_shape=jax.ShapeDtypeStruct((t, d), BF16),
        grid=(t // tm,),
        in_specs=[pl.BlockSpec((tm, d), lambda i: (i, 0)), pl.BlockSpec((1, d), lambda i: (0, 0))],
        out_specs=pl.BlockSpec((tm, d), lambda i: (i, 0)),
        compiler_params=_cparams(("parallel",)),
        name="prenorm",
    )(x2d, gain.reshape(1, d).astype(F32))


def _postnorm_kernel(x_ref, y_ref, g_ref, *rest):
    y = y_ref[...].astype(F32)
    ms = jnp.mean(y * y, axis=-1, keepdims=True)
    xn = x_ref[...] + y * lax.rsqrt(ms + NORM_EPS) * g_ref[...]
    if len(rest) == 1:
        rest[0][...] = xn
    else:
        gn_ref, o_ref, h_ref = rest
        o_ref[...] = xn
        ms2 = jnp.mean(xn * xn, axis=-1, keepdims=True)
        h_ref[...] = (xn * lax.rsqrt(ms2 + NORM_EPS) * gn_ref[...]).astype(h_ref.dtype)


def _postnorm_residual(x2d, y2d, gain, next_gain=None, tm=256):
    t, d = x2d.shape
    row = pl.BlockSpec((tm, d), lambda i: (i, 0))
    vec = pl.BlockSpec((1, d), lambda i: (0, 0))
    operands = [x2d, y2d, gain.reshape(1, d).astype(F32)]
    in_specs = [row, row, vec]
    out_shape = jax.ShapeDtypeStruct((t, d), F32)
    out_specs = row
    if next_gain is not None:
        operands.append(next_gain.reshape(1, d).astype(F32))
        in_specs.append(vec)
        out_shape = (out_shape, jax.ShapeDtypeStruct((t, d), BF16))
        out_specs = (row, row)
    res = pl.pallas_call(
        _postnorm_kernel,
        out_shape=out_shape,
        grid=(t // tm,),
        in_specs=in_specs,
        out_specs=out_specs,
        compiler_params=_cparams(("parallel",)),
        name="postnorm_residual",
    )(*operands)
    return res if next_gain is not None else (res, None)


PROJ_SHIFT_ROWS = 256


def _proj_kernel(a_ref, w_ref, *rest, lane_shift):
    if lane_shift:
        wn_ref, o_ref, wb_ref = rest
    else:
        o_ref, wb_ref = rest

    @pl.when(pl.program_id(1) == 0)
    def _():
        if lane_shift:
            kdim, tn = wb_ref.shape

            def body(c, carry):
                r = pl.multiple_of(c * PROJ_SHIFT_ROWS, PROJ_SHIFT_ROWS)
                both = jnp.concatenate([w_ref[pl.ds(r, PROJ_SHIFT_ROWS), :], wn_ref[pl.ds(r, PROJ_SHIFT_ROWS), :]],
                                       axis=1)
                moved = pltpu.roll(both, tn + LANES - lane_shift, 1)
                wb_ref[pl.ds(r, PROJ_SHIFT_ROWS), :] = moved[:, :tn].astype(BF16)
                return carry

            lax.fori_loop(0, kdim // PROJ_SHIFT_ROWS, body, 0)
        else:
            wb_ref[...] = w_ref[...].astype(BF16)

    o_ref[...] = jnp.dot(a_ref[...], wb_ref[...], preferred_element_type=F32).astype(o_ref.dtype)


def _proj(a, w, out_dtype, col0=0, ncols=None, layer=0, lane_shift=0, tm=1024, tn=512):
    if w.ndim == 2:
        w = w.reshape((1,) + w.shape)
    m, kdim = a.shape
    ncols = w.shape[2] - col0 if ncols is None else ncols
    tm, tn = min(tm, m), min(tn, ncols)
    assert m % tm == 0 and ncols % tn == 0 and col0 % tn == 0 and w.shape[1] == kdim
    assert 0 <= lane_shift < LANES and kdim % PROJ_SHIFT_ROWS == 0
    cb0 = col0 // tn
    in_specs = [pl.BlockSpec((tm, kdim), lambda j, i: (i, 0)),
                pl.BlockSpec((None, kdim, tn), lambda j, i: (layer, 0, cb0 + j))]
    operands = [a, w]
    if lane_shift:
        lane_tiles = tn // LANES
        in_specs.append(pl.BlockSpec((None, kdim, LANES), lambda j, i: (layer, 0, (cb0 + j + 1) * lane_tiles)))
        operands.append(w)
    return pl.pallas_call(
        functools.partial(_proj_kernel, lane_shift=lane_shift),
        out_shape=jax.ShapeDtypeStruct((m, ncols), out_dtype),
        grid=(ncols // tn, m // tm),
        in_specs=in_specs,
        out_specs=pl.BlockSpec((tm, tn), lambda j, i: (i, j)),
        scratch_shapes=[pltpu.VMEM((kdim, tn), BF16)],
        compiler_params=_cparams(("parallel", "arbitrary")),
        name="projection",
    )(*operands)


def _gelu_tanh(x):
    return 0.5 * x * (1.0 + jnp.tanh(math.sqrt(2.0 / math.pi) * (x + 0.044715 * (x * x * x))))


def _compress_kernel(x_ref, pe_ref, w1_ref, w2_ref, o_ref, xf_ref):
    s = x_ref.shape[0]
    nch = s // CMP_STRIDE
    xf_ref[...] = x_ref[...].astype(F32)
    acc_lo = jnp.zeros((nch, HEAD_DIM), F32)
    acc_hi = jnp.zeros((nch, HEAD_DIM), F32)
    for l in range(CMP_STRIDE):
        xl = xf_ref[pl.ds(l, nch, stride=CMP_STRIDE), :]
        lo_in = (xl + pe_ref[l:l + 1, :]).astype(BF16)
        hi_in = (xl + pe_ref[CMP_STRIDE + l:CMP_STRIDE + l + 1, :]).astype(BF16)
        acc_lo += jnp.dot(lo_in, w1_ref[l * HEAD_DIM:(l + 1) * HEAD_DIM, :], preferred_element_type=F32)
        acc_hi += jnp.dot(hi_in, w1_ref[(CMP_STRIDE + l) * HEAD_DIM:(CMP_STRIDE + l + 1) * HEAD_DIM, :],
                          preferred_element_type=F32)
    pre = acc_lo + pltpu.roll(acc_hi, nch - 1, 0)
    hid = _gelu_tanh(pre).astype(BF16)
    o_ref[...] = jnp.dot(hid, w2_ref[...], preferred_element_type=F32).astype(o_ref.dtype)


def _nsa_compress(qkv, pe, w1, w2):
    bsz, s, _ = qkv.shape
    nch = s // CMP_STRIDE
    kv_block0 = NSA_HEADS
    return pl.pallas_call(
        _compress_kernel,
        out_shape=jax.ShapeDtypeStruct((bsz, 2, NSA_GROUPS, nch, HEAD_DIM), BF16),
        grid=(bsz, 2, NSA_GROUPS),
        in_specs=[
            pl.BlockSpec((None, s, HEAD_DIM), lambda b, w, g: (b, 0, kv_block0 + w * NSA_GROUPS + g)),
            pl.BlockSpec((None, CMP_LEN, HEAD_DIM), lambda b, w, g: (w, 0, 0)),
            pl.BlockSpec((None, CMP_LEN * HEAD_DIM, HEAD_DIM), lambda b, w, g: (w, 0, 0)),
            pl.BlockSpec((None, HEAD_DIM, HEAD_DIM), lambda b, w, g: (w, 0, 0)),
        ],
        out_specs=pl.BlockSpec((None, None, None, nch, HEAD_DIM), lambda b, w, g: (b, w, g, 0, 0)),
        scratch_shapes=[pltpu.VMEM((s, HEAD_DIM), F32)],
        compiler_params=_cparams(("parallel", "parallel", "parallel")),
        name="nsa_compress",
    )(qkv, pe.astype(F32), w1.astype(BF16), w2.astype(BF16))


def _split3(x):
    hi = x.astype(BF16)
    r1 = x - hi.astype(F32)
    mid = r1.astype(BF16)
    lo = (r1 - mid.astype(F32)).astype(BF16)
    return hi, mid, lo


def _softmax_rows(s):
    m = jnp.max(s, axis=-1, keepdims=True)
    p = jnp.exp(s - m)
    return p, jnp.sum(p, axis=-1, keepdims=True)


def _nsa_attn_kernel(slopes_ref, slx_ref, kx_ref, q_ref, kcmp_ref, vcmp_ref, ks_ref, vs_ref, kw_ref, vw_ref,
                     gl_ref, z_ref, y_ref, ksx_ref, kwx_ref, qx_ref, oc_ref, *, tq, seq):
    qi = pl.program_id(2)
    t0 = qi * tq
    scale = HEAD_DIM ** -0.5
    n_cmp_pad = kcmp_ref.shape[0]
    n_sel = seq // SEL_LEN
    rows = NSA_HPG * tq
    nt = (((1,), (1,)), ((), ()))

    @pl.when(qi == 0)
    def _():
        kx = kx_ref[...]
        lane_k = lax.broadcasted_iota(jnp.int32, kx.shape, 1)
        ksx_ref[:, 0:HEAD_DIM] = ks_ref[...]
        ksx_ref[:, HEAD_DIM:] = kx
        kwx_ref[:, 0:HEAD_DIM] = kw_ref[...]
        kwx_ref[:, HEAD_DIM:] = jnp.where(lane_k >= n_sel, kx, jnp.zeros_like(kx))

    tpos_i = t0 + lax.broadcasted_iota(jnp.int32, (tq, 1), 0)
    tpos = tpos_i.astype(F32)

    cmp_end = (lax.broadcasted_iota(jnp.int32, (1, n_cmp_pad), 1) * CMP_STRIDE + (CMP_LEN - 1)).astype(F32)
    dist_c = tpos - cmp_end
    valid_c = dist_c >= 0.0
    row_valid = tpos >= float(CMP_LEN - 1)
    kcmp = kcmp_ref[...]
    vcmp = vcmp_ref[...]
    pg = jnp.zeros((tq, n_cmp_pad), F32)
    for j in range(NSA_HPG):
        qh = q_ref[:, j * HEAD_DIM:(j + 1) * HEAD_DIM]
        sl = slopes_ref[j:j + 1, :]
        s = lax.dot_general(qh, kcmp, nt, preferred_element_type=F32) * scale - sl * dist_c
        s = jnp.where(valid_c, s, NEG_INF)
        p, l = _softmax_rows(s)
        p = jnp.where(valid_c & row_valid, p * (1.0 / l), 0.0)
        pg = pg + p
        oc_ref[j * tq:(j + 1) * tq, :] = jnp.dot(p.astype(BF16), vcmp, preferred_element_type=F32)

    n_idx = lax.broadcasted_iota(jnp.int32, (n_cmp_pad, LANES), 0)
    j_idx = lax.broadcasted_iota(jnp.int32, (n_cmp_pad, LANES), 1)
    dd = n_idx - (SEL_LEN // CMP_STRIDE) * j_idx + (CMP_LEN // CMP_STRIDE - 1)
    pool = jnp.where((dd == 0) | (dd == 4), 1.0, jnp.where((dd >= 1) & (dd <= 3), 2.0, 0.0)).astype(BF16)
    p_slc = jnp.zeros((tq, LANES), F32)
    for part in _split3(pg):
        p_slc = p_slc + jnp.dot(part, pool, preferred_element_type=F32)

    blk = lax.broadcasted_iota(jnp.int32, (1, LANES), 1)
    cur = lax.shift_right_logical(tpos_i, SEL_SHIFT)
    forced = (blk == 0) | (blk == cur) | (blk == cur - 1)
    future = blk > cur
    score = jnp.where(forced, FORCE_SCORE, jnp.where(future, -1.0, p_slc))
    score = jnp.where(blk < n_sel, score, -2.0)
    score_t = score.T[0:n_sel, :]
    blk_t = lax.broadcasted_iota(jnp.int32, (n_sel, 1), 0)
    rank = jnp.zeros((n_sel, tq), F32)
    for jp in range(n_sel):
        other = score_t[jp:jp + 1, :]
        ahead = (other > score_t) | ((other == score_t) & (blk_t > jp))
        rank = rank + jnp.where(ahead, 1.0, 0.0)
    pen_t = jnp.where(rank < float(min(SEL_TOPK, n_sel)), 0.0, -NSA_MASK_BIG)
    penalty = jnp.concatenate([pen_t, jnp.zeros((LANES - n_sel, tq), F32)], axis=0).T

    for j in range(NSA_HPG):
        qx_ref[j * tq:(j + 1) * tq, 0:HEAD_DIM] = q_ref[:, j * HEAD_DIM:(j + 1) * HEAD_DIM]
        qx_ref[j * tq:(j + 1) * tq, HEAD_DIM:] = (penalty + slx_ref[j:j + 1, :]).astype(BF16)

    t_loc = lax.broadcasted_iota(jnp.int32, (tq, 1), 0)
    k_loc = lax.broadcasted_iota(jnp.int32, (1, tq), 1)
    causal = k_loc <= t_loc
    far_ok = k_loc > t_loc
    n_back = WINDOW // tq
    exp2_scale = scale * math.log2(math.e)
    gates = _sigmoid(gl_ref[...].astype(F32))
    lane_g = lax.broadcasted_iota(jnp.int32, (1, LANES), 1)
    head0 = pl.program_id(1) * NSA_HPG

    def attend(qx, pieces):
        ss = []
        for kx_p, _, mask in pieces:
            s = lax.dot_general(qx, kx_p, nt, preferred_element_type=F32)
            ss.append(s if mask is None else jnp.where(mask, s, NEG_INF))
        m = functools.reduce(jnp.maximum, [jnp.max(s, axis=-1, keepdims=True) for s in ss])
        l = 0.0
        o = 0.0
        for s, (_, v_p, _) in zip(ss, pieces):
            p = jnp.exp2((s - m) * exp2_scale)
            l = l + jnp.sum(p, axis=-1, keepdims=True)
            o = o + jnp.dot(p.astype(BF16), v_p, preferred_element_type=F32)
        return o * (1.0 / l)

    def variant(n):
        d0 = (n - 1) * tq

        def head_body(j, carry):
            r0 = pl.multiple_of(j * tq, tq)
            c0 = pl.multiple_of(j * HEAD_DIM, HEAD_DIM)
            qx = qx_ref[pl.ds(r0, tq), :]
            sel_pieces = [(ksx_ref[d0:d0 + tq, :], vs_ref[d0:d0 + tq, :], causal)]
            if n > 1:
                sel_pieces.append((ksx_ref[0:d0, :], vs_ref[0:d0, :], None))
            o_slc = attend(qx, sel_pieces)
            win_pieces = [(kwx_ref[d0:d0 + tq, :], vw_ref[d0:d0 + tq, :], causal)]
            for w in range(1, min(n - 1, n_back) + 1):
                k0 = d0 - w * tq
                win_pieces.append((kwx_ref[k0:k0 + tq, :], vw_ref[k0:k0 + tq, :], far_ok if w == n_back else None))
            o_win = attend(qx, win_pieces)

            def gate(branch):
                pick = lane_g == branch * NSA_HEADS + head0 + j
                return jnp.sum(jnp.where(pick, gates, 0.0), axis=-1, keepdims=True)

            o = gate(0) * oc_ref[pl.ds(r0, tq), :] + gate(1) * o_slc + gate(2) * o_win
            zj = z_ref[:, pl.ds(c0, HEAD_DIM)].astype(F32)
            y_ref[:, pl.ds(c0, HEAD_DIM)] = (o * _silu(zj)).astype(y_ref.dtype)
            return carry

        lax.fori_loop(0, NSA_HPG, head_body, 0, unroll=2)

    for n in range(1, seq // tq + 1):
        pl.when(qi == n - 1)(functools.partial(variant, n))


def _alibi_slopes():
    return 2.0 ** (-ALIBI_MAX_EXP * jnp.arange(1, NSA_HEADS + 1, dtype=F32) / NSA_HEADS)


def _nsa_tables(seq):
    slopes = _alibi_slopes()
    slope_rows = jnp.broadcast_to(slopes.reshape(NSA_GROUPS, NSA_HPG, 1), (NSA_GROUPS, NSA_HPG, LANES))
    n_sel = seq // SEL_LEN
    parts = []
    rest = slopes * (HEAD_DIM ** 0.5)
    for _ in range(3):
        part = rest.astype(BF16).astype(F32)
        parts.append(part)
        rest = rest - part
    slx = jnp.zeros((NSA_HEADS, LANES), F32)
    for c, part in enumerate(parts + parts):
        slx = slx.at[:, n_sel + c].set(part)
    slx = slx.reshape(NSA_GROUPS, NSA_HPG, LANES)
    key = jnp.arange(seq, dtype=jnp.int32)
    lane = jnp.arange(LANES, dtype=jnp.int32)[None, :]
    hi = ((key // SEL_LEN) * SEL_LEN).astype(F32)[:, None]
    lo = (key % SEL_LEN).astype(F32)[:, None]
    kx = jnp.where(lane == (key // SEL_LEN)[:, None], 1.0, 0.0)
    kx = jnp.where((lane >= n_sel) & (lane < n_sel + 3), hi, kx)
    kx = jnp.where((lane >= n_sel + 3) & (lane < n_sel + 6), lo, kx)
    return slope_rows, slx, kx.astype(BF16)


def _nsa_attention(qkv, kv_cmp, gl, z, tq=256):
    bsz, s, _ = qkv.shape
    gw = NSA_HPG * HEAD_DIM
    ncp = kv_cmp.shape[3]
    kvb = NSA_HEADS
    rows = NSA_HPG * tq
    assert WINDOW % tq == 0 and s // SEL_LEN + 6 <= LANES and tq % SEL_LEN == 0
    slope_rows, slx, kx = _nsa_tables(s)

    def kv_spec(which):
        return pl.BlockSpec((None, s, HEAD_DIM), lambda b, g, i: (b, 0, kvb + which * NSA_GROUPS + g))

    return pl.pallas_call(
        functools.partial(_nsa_attn_kernel, tq=tq, seq=s),
        out_shape=jax.ShapeDtypeStruct((bsz, s, NSA_HEADS * HEAD_DIM), BF16),
        grid=(bsz, NSA_GROUPS, s // tq),
        in_specs=[
            pl.BlockSpec((None, NSA_HPG, LANES), lambda b, g, i: (g, 0, 0)),
            pl.BlockSpec((None, NSA_HPG, LANES), lambda b, g, i: (g, 0, 0)),
            pl.BlockSpec((s, LANES), lambda b, g, i: (0, 0)),
            pl.BlockSpec((None, tq, gw), lambda b, g, i: (b, i, g)),
            pl.BlockSpec((None, None, None, ncp, HEAD_DIM), lambda b, g, i: (b, 0, g, 0, 0)),
            pl.BlockSpec((None, None, None, ncp, HEAD_DIM), lambda b, g, i: (b, 1, g, 0, 0)),
            kv_spec(2), kv_spec(3), kv_spec(4), kv_spec(5),
            pl.BlockSpec((None, tq, LANES), lambda b, g, i: (b, i, 0)),
            pl.BlockSpec((None, tq, gw), lambda b, g, i: (b, i, g)),
        ],
        out_specs=pl.BlockSpec((None, tq, gw), lambda b, g, i: (b, i, g)),
        scratch_shapes=[
            pltpu.VMEM((s, 2 * HEAD_DIM), BF16),
            pltpu.VMEM((s, 2 * HEAD_DIM), BF16),
            pltpu.VMEM((rows, 2 * HEAD_DIM), BF16),
            pltpu.VMEM((rows, HEAD_DIM), F32),
        ],
        compiler_params=_cparams(("parallel", "parallel", "arbitrary")),
        name="nsa_attention",
    )(slope_rows, slx, kx, qkv, kv_cmp, kv_cmp, qkv, qkv, qkv, qkv, gl, z)


def _nsa_mixer(h2d, bsz, seq, layer, w_in, cmp_pe, cmp_w1, cmp_w2, w_out):
    inner = NSA_HEADS * HEAD_DIM
    kv = NSA_GROUPS * HEAD_DIM
    n_qkv = inner + 6 * kv
    n_gl = 3 * NSA_HEADS
    assert n_gl < LANES and n_qkv % LANES == 0
    qkv = _proj(h2d, w_in, BF16, 0, n_qkv, layer).reshape(bsz, seq, n_qkv)
    gl = _proj(h2d, w_in, F32, n_qkv, LANES, layer).reshape(bsz, seq, LANES)
    z = _proj(h2d, w_in, BF16, n_qkv, inner, layer, lane_shift=n_gl).reshape(bsz, seq, inner)
    kv_cmp = _nsa_compress(qkv, cmp_pe[layer], cmp_w1[layer], cmp_w2[layer])
    y = _nsa_attention(qkv, kv_cmp, gl, z)
    return _proj(y.reshape(bsz * seq, inner), w_out, BF16, layer=layer)


def _rg_kernel(xb_ref, z_ref, cw_ref, cb_ref, gw_ref, gb_ref, lam_ref, y_ref, xpad_ref, h_ref, *, ts):
    si = pl.program_id(2)
    halo = SUBLANES

    @pl.when(si == 0)
    def _():
        xpad_ref[0:halo, :] = jnp.zeros((halo, xpad_ref.shape[1]), F32)
        h_ref[...] = jnp.zeros_like(h_ref)

    x = xb_ref[...]
    xpad_ref[halo:halo + ts, :] = x
    xc = cb_ref[...] + cw_ref[RG_CONV - 1:RG_CONV, :] * x
    for k in range(RG_CONV - 1):
        shift = RG_CONV - 1 - k
        xc = xc + cw_ref[k:k + 1, :] * xpad_ref[halo - shift:halo - shift + ts, :]
    xpad_ref[0:halo, :] = x[ts - halo:ts, :]

    xcb = xc.astype(BF16)
    gate_i = _sigmoid(jnp.dot(xcb, gw_ref[0], preferred_element_type=F32) + gb_ref[0])
    gate_r = _sigmoid(jnp.dot(xcb, gw_ref[1], preferred_element_type=F32) + gb_ref[1])
    nl = -lam_ref[...]
    softplus = jnp.maximum(nl, 0.0) + jnp.log(1.0 + jnp.exp(-jnp.abs(nl)))
    log_a = (-RG_C) * gate_r * softplus
    a = jnp.exp(log_a)
    mult = jnp.sqrt(1.0 - a * a)
    row = lax.broadcasted_iota(jnp.int32, (ts, 1), 0)
    mult = jnp.where((row + si * ts) == 0, 1.0, mult)
    u = mult * gate_i * xc

    d = 1
    while d < ts:
        keep = row >= d
        a_sh = pltpu.roll(a, d, 0)
        u_sh = pltpu.roll(u, d, 0)
        u = jnp.where(keep, a * u_sh + u, u)
        a = jnp.where(keep, a * a_sh, a)
        d *= 2
    hs = a * h_ref[...] + u
    h_ref[...] = hs[ts - 1:ts, :]
    y_ref[...] = (hs * _silu(z_ref[...].astype(F32))).astype(y_ref.dtype)


def _rg_core(xb, z, conv_w, conv_b, gate_w, gate_b, lam, ts=512):
    bsz, s, w = xb.shape
    cb = w // RG_BLOCKS
    return pl.pallas_call(
        functools.partial(_rg_kernel, ts=ts),
        out_shape=jax.ShapeDtypeStruct((bsz, s, w), BF16),
        grid=(bsz, RG_BLOCKS, s // ts),
        in_specs=[
            pl.BlockSpec((None, ts, cb), lambda b, n, i: (b, i, n)),
            pl.BlockSpec((None, ts, cb), lambda b, n, i: (b, i, n)),
            pl.BlockSpec((RG_CONV, cb), lambda b, n, i: (0, n)),
            pl.BlockSpec((1, cb), lambda b, n, i: (0, n)),
            pl.BlockSpec((2, None, cb, cb), lambda b, n, i: (0, n, 0, 0)),
            pl.BlockSpec((2, None, 1, cb), lambda b, n, i: (0, n, 0, 0)),
            pl.BlockSpec((1, cb), lambda b, n, i: (0, n)),
        ],
        out_specs=pl.BlockSpec((None, ts, cb), lambda b, n, i: (b, i, n)),
        scratch_shapes=[pltpu.VMEM((ts + SUBLANES, cb), F32), pltpu.VMEM((1, cb), F32)],
        compiler_params=_cparams(("parallel", "parallel", "arbitrary")),
        name="rglru_core",
    )(xb, z, conv_w.astype(F32), conv_b.reshape(1, w).astype(F32), gate_w.astype(BF16),
      gate_b.reshape(2, RG_BLOCKS, 1, cb).astype(F32), lam.reshape(1, w).astype(F32))


def _rglru_mixer(h2d, bsz, seq, w_in, conv_w, conv_b, gate_w, gate_b, lam, w_out):
    width = w_out.shape[0]
    xb = _proj(h2d, w_in, F32, 0, width).reshape(bsz, seq, width)
    z = _proj(h2d, w_in, BF16, width, width).reshape(bsz, seq, width)
    y = _rg_core(xb, z, conv_w, conv_b, gate_w, gate_b, lam)
    return _proj(y.reshape(bsz * seq, width), w_out, BF16)


def _hg_kernel(q_ref, f_ref, v_ref, g_ref, lbl_ref, ng_ref, y_ref, state_ref, b_ref, k_ref, *, tc, layer):
    ci = pl.program_id(2)
    dk = q_ref.shape[1]
    nt = (((1,), (1,)), ((), ()))
    tn = (((0,), (0,)), ((), ()))

    @pl.when(ci == 0)
    def _():
        state_ref[...] = jnp.zeros_like(state_ref)

    lg = lbl_ref[...]
    e = jnp.exp(lg - jnp.max(lg, axis=0, keepdims=True))
    pl_sm = e * (1.0 / jnp.sum(e, axis=0, keepdims=True))
    lb = jnp.zeros((1, dk), F32)
    for r in range(1, layer + 1):
        lb = lb + pl_sm[r:r + 1, :]

    q = _silu(q_ref[...])
    fg = lb + (1.0 - lb) * _sigmoid(f_ref[...])
    kk = 1.0 - fg
    b = jnp.log2(fg)
    row = lax.broadcasted_iota(jnp.int32, (tc, 1), 0)
    rc = row & (HG_CHUNK - 1)
    d = 1
    while d < HG_CHUNK:
        b = b + jnp.where(rc >= d, pltpu.roll(b, d, 0), 0.0)
        d *= 2
    b_ref[...] = b
    k_ref[...] = kk

    nsub = tc // HG_SUB
    lane = lax.broadcasted_iota(jnp.int32, (dk, LANES), 1)
    acc = jnp.zeros((tc, LANES), F32)
    for s in range(HG_SUB):
        b_s = jnp.concatenate(
            [jnp.broadcast_to(b_ref[i * HG_SUB + s:i * HG_SUB + s + 1, :], (HG_SUB, dk)) for i in range(nsub)], axis=0)
        k_s = jnp.concatenate(
            [jnp.broadcast_to(k_ref[i * HG_SUB + s:i * HG_SUB + s + 1, :], (HG_SUB, dk)) for i in range(nsub)], axis=0)
        m_s = q * jnp.exp2(jnp.minimum(b - b_s, 0.0)) * k_s
        w_s = jnp.where(((lane & (HG_SUB - 1)) == s) & (lane < HG_CHUNK), 1.0, 0.0).astype(BF16)
        acc = acc + jnp.dot(m_s.astype(BF16), w_s, preferred_element_type=F32)

    col = lax.broadcasted_iota(jnp.int32, (HG_CHUNK, HG_CHUNK), 1)
    rw = lax.broadcasted_iota(jnp.int32, (HG_CHUNK, HG_CHUNK), 0)
    col_sub = lax.shift_right_logical(col, HG_SUB_SHIFT)
    rw_sub = lax.shift_right_logical(rw, HG_SUB_SHIFT)
    diag_mask = (col_sub == rw_sub) & (col <= rw)

    state = state_ref[...]
    nsc = HG_CHUNK // HG_SUB
    for c in range(tc // HG_CHUNK):
        r0 = c * HG_CHUNK
        bc = b[r0:r0 + HG_CHUNK, :]
        qc = q[r0:r0 + HG_CHUNK, :]
        kc = kk[r0:r0 + HG_CHUNK, :]
        vc = v_ref[r0:r0 + HG_CHUNK, :]
        o = lax.dot_general((qc * jnp.exp2(bc)).astype(BF16), state.astype(BF16), nt, preferred_element_type=F32)
        blocks = [jnp.zeros((HG_SUB, HG_CHUNK), F32)]
        for i in range(1, nsc):
            r_i = bc[i * HG_SUB - 1:i * HG_SUB, :]
            q_i = (qc[i * HG_SUB:(i + 1) * HG_SUB, :] * jnp.exp2(bc[i * HG_SUB:(i + 1) * HG_SUB, :] - r_i)).astype(BF16)
            k_i = (kc * jnp.exp2(jnp.minimum(r_i - bc, 0.0))).astype(BF16)
            blocks.append(lax.dot_general(q_i, k_i, nt, preferred_element_type=F32))
        att_off = jnp.concatenate(blocks, axis=0)
        att = jnp.where(diag_mask, acc[r0:r0 + HG_CHUNK, 0:HG_CHUNK],
                        jnp.where(col_sub < rw_sub, att_off, 0.0))
        o = o + jnp.dot(att.astype(BF16), vc, preferred_element_type=F32)
        b_last = bc[HG_CHUNK - 1:HG_CHUNK, :]
        k_dec = (kc * jnp.exp2(b_last - bc)).astype(BF16)
        state = state * jnp.exp2(b_last) + lax.dot_general(vc, k_dec, tn, preferred_element_type=F32)
        ms = jnp.mean(o * o, axis=-1, keepdims=True)
        on = o * lax.rsqrt(ms + NORM_EPS) * ng_ref[...]
        gc = g_ref[r0:r0 + HG_CHUNK, :].astype(F32)
        y_ref[r0:r0 + HG_CHUNK, :] = (on * _silu(gc)).astype(y_ref.dtype)
    state_ref[...] = state


def _hg_core(qf, vg, lb_logits, norm_gain, layer, tc=512):
    bsz, s, _ = qf.shape
    dk = qf.shape[2] // (2 * HG_HEADS)
    dv = vg.shape[2] // (2 * HG_HEADS)
    nl = lb_logits.shape[0]
    return pl.pallas_call(
        functools.partial(_hg_kernel, tc=tc, layer=layer),
        out_shape=jax.ShapeDtypeStruct((bsz, s, HG_HEADS * dv), BF16),
        grid=(bsz, HG_HEADS, s // tc),
        in_specs=[
            pl.BlockSpec((None, tc, dk), lambda b, h, i: (b, i, h)),
            pl.BlockSpec((None, tc, dk), lambda b, h, i: (b, i, HG_HEADS + h)),
            pl.BlockSpec((None, tc, dv), lambda b, h, i: (b, i, h)),
            pl.BlockSpec((None, tc, dv), lambda b, h, i: (b, i, HG_HEADS + h)),
            pl.BlockSpec((nl, dk), lambda b, h, i: (0, h)),
            pl.BlockSpec((1, dv), lambda b, h, i: (0, 0)),
        ],
        out_specs=pl.BlockSpec((None, tc, dv), lambda b, h, i: (b, i, h)),
        scratch_shapes=[pltpu.VMEM((dv, dk), F32), pltpu.VMEM((tc, dk), F32), pltpu.VMEM((tc, dk), F32)],
        compiler_params=_cparams(("parallel", "parallel", "arbitrary")),
        name="hgrn2_core",
    )(qf, qf, vg, vg, lb_logits.astype(F32), norm_gain.reshape(1, dv).astype(F32))


def _hgrn2_mixer(h2d, bsz, seq, w_in, lb_logits, layer, norm_gain, w_out):
    val = w_out.shape[0]
    key = (w_in.shape[1] - 2 * val) // 2
    qf = _proj(h2d, w_in, F32, 0, 2 * key).reshape(bsz, seq, 2 * key)
    vg = _proj(h2d, w_in, BF16, 2 * key, 2 * val).reshape(bsz, seq, 2 * val)
    y = _hg_core(qf, vg, lb_logits, norm_gain, layer)
    return _proj(y.reshape(bsz * seq, val), w_out, BF16)


def kernel(x, pre_norm_gain, post_norm_gain, nsa_w_in, nsa_cmp_pe, nsa_cmp_w1, nsa_cmp_w2, nsa_w_out,
           rg_w_in, rg_conv_w, rg_conv_b, rg_gate_w, rg_gate_b, rg_lambda, rg_w_out,
           hg_w_in, hg_lb_logits, hg_norm_gain, hg_w_out):
    bsz, seq, d = x.shape
    depth = pre_norm_gain.shape[0]
    x2d = x.reshape(bsz * seq, d)
    h = _prenorm(x2d, pre_norm_gain[0])
    for i in range(depth):
        kind, j = i % 3, i // 3
        if kind == 0:
            y = _nsa_mixer(h, bsz, seq, j, nsa_w_in, nsa_cmp_pe, nsa_cmp_w1, nsa_cmp_w2, nsa_w_out)
        elif kind == 1:
            y = _rglru_mixer(h, bsz, seq, rg_w_in[j], rg_conv_w[j], rg_conv_b[j], rg_gate_w[j], rg_gate_b[j],
                             rg_lambda[j], rg_w_out[j])
        else:
            y = _hgrn2_mixer(h, bsz, seq, hg_w_in[j], hg_lb_logits, i, hg_norm_gain[j], hg_w_out[j])
        next_gain = pre_norm_gain[i + 1] if i + 1 < depth else None
        x2d, h = _postnorm_residual(x2d, y, post_norm_gain[i], next_gain)
    return x2d.reshape(bsz, seq, d)
```

```python
import functools
import math

import jax
import jax.numpy as jnp
from jax import lax
from jax.experimental import pallas as pl
from jax.experimental.pallas import tpu as pltpu

F32 = jnp.float32
BF16 = jnp.bfloat16

NORM_EPS = 1e-6
NEG_INF = -1e30
FORCE_SCORE = 1e6

LANES = 128
SUBLANES = 8
V7X_VMEM_LIMIT_BYTES = 56 * 1024 * 1024

NSA_HEADS = 32
NSA_GROUPS = 4
NSA_HPG = NSA_HEADS // NSA_GROUPS
HEAD_DIM = 128
CMP_LEN = 32
CMP_STRIDE = 16
SEL_LEN = 64
SEL_SHIFT = 6
SEL_TOPK = 16
WINDOW = 512
ALIBI_MAX_EXP = 8.0
NSA_MASK_BIG = 2.0 ** 40

RG_BLOCKS = 16
RG_CONV = 4
RG_C = 8.0

HG_HEADS = 32
HG_CHUNK = 64
HG_SUB = 8
HG_SUB_SHIFT = 3


def _cparams(sem):
    return pltpu.CompilerParams(dimension_semantics=sem, vmem_limit_bytes=V7X_VMEM_LIMIT_BYTES)


def _sigmoid(x):
    return 0.5 * jnp.tanh(0.5 * x) + 0.5


def _silu(x):
    return x * _sigmoid(x)


def _prenorm_kernel(x_ref, g_ref, o_ref):
    x = x_ref[...]
    ms = jnp.mean(x * x, axis=-1, keepdims=True)
    o_ref[...] = (x * lax.rsqrt(ms + NORM_EPS) * g_ref[...]).astype(o_ref.dtype)


def _prenorm(x2d, gain, tm=256):
    t, d = x2d.shape
    return pl.pallas_call(
        _prenorm_kernel,
        out_shape=jax.ShapeDtypeStruct((t, d), BF16),
        grid=(t // tm,),
        in_specs=[pl.BlockSpec((tm, d), lambda i: (i, 0)), pl.BlockSpec((1, d), lambda i: (0, 0))],
        out_specs=pl.BlockSpec((tm, d), lambda i: (i, 0)),
        compiler_params=_cparams(("parallel",)),
        name="prenorm",
    )(x2d, gain.reshape(1, d).astype(F32))


def _postnorm_kernel(x_ref, y_ref, g_ref, *rest):
    y = y_ref[...].astype(F32)
    ms = jnp.mean(y * y, axis=-1, keepdims=True)
    xn = x_ref[...] + y * lax.rsqrt(ms + NORM_EPS) * g_ref[...]
    if len(rest) == 1:
        rest[0][...] = xn
    else:
        gn_ref, o_ref, h_ref = rest
        o_ref[...] = xn
        ms2 = jnp.mean(xn * xn, axis=-1, keepdims=True)
        h_ref[...] = (xn * lax.rsqrt(ms2 + NORM_EPS) * gn_ref[...]).astype(h_ref.dtype)


def _postnorm_residual(x2d, y2d, gain, next_gain=None, tm=256):
    t, d = x2d.shape
    row = pl.BlockSpec((tm, d), lambda i: (i, 0))
    vec = pl.BlockSpec((1, d), lambda i: (0, 0))
    operands = [x2d, y2d, gain.reshape(1, d).astype(F32)]
    in_specs = [row, row, vec]
    out_shape = jax.ShapeDtypeStruct((t, d), F32)
    out_specs = row
    if next_gain is not None:
        operands.append(next_gain.reshape(1, d).astype(F32))
        in_specs.append(vec)
        out_shape = (out_shape, jax.ShapeDtypeStruct((t, d), BF16))
        out_specs = (row, row)
    res = pl.pallas_call(
        _postnorm_kernel,
        out_shape=out_shape,
        grid=(t // tm,),
        in_specs=in_specs,
        out_specs=out_specs,
        compiler_params=_cparams(("parallel",)),
        name="postnorm_residual",
    )(*operands)
    return res if next_gain is not None else (res, None)


def _proj_kernel(a_ref, w_ref, *rest, shift, transposed):
    if shift:
        wn_ref, o_ref, wb_ref = rest
    else:
        o_ref, wb_ref = rest

    @pl.when(pl.program_id(1) == 0)
    def _():
        if shift:
            tn = wb_ref.shape[0]
            wb_ref[0:tn - shift, :] = w_ref[shift:tn, :].astype(BF16)
            wb_ref[tn - shift:tn, :] = wn_ref[0:shift, :].astype(BF16)
        else:
            wb_ref[...] = w_ref[...].astype(BF16)

    if transposed:
        acc = lax.dot_general(a_ref[...], wb_ref[...], (((1,), (1,)), ((), ())), preferred_element_type=F32)
    else:
        acc = jnp.dot(a_ref[...], wb_ref[...], preferred_element_type=F32)
    o_ref[...] = acc.astype(o_ref.dtype)


def _proj(a, w, out_dtype, col0=0, ncols=None, layer=0, shift=0, transposed=False, tm=1024, tn=512):
    if w.ndim == 2:
        w = w.reshape((1,) + w.shape)
    m, kdim = a.shape
    k_axis, n_axis = (2, 1) if transposed else (1, 2)
    ncols = w.shape[n_axis] - col0 if ncols is None else ncols
    tm, tn = min(tm, m), min(tn, ncols)
    assert m % tm == 0 and ncols % tn == 0 and col0 % tn == 0 and w.shape[k_axis] == kdim
    assert shift == 0 or (transposed and 0 < shift < LANES and shift % (2 * SUBLANES) == 0 and tn % LANES == 0)
    cb0 = col0 // tn
    if transposed:
        w_block = (None, tn, kdim)
        w_spec = pl.BlockSpec(w_block, lambda j, i: (layer, cb0 + j, 0))
    else:
        w_block = (None, kdim, tn)
        w_spec = pl.BlockSpec(w_block, lambda j, i: (layer, 0, cb0 + j))
    in_specs = [pl.BlockSpec((tm, kdim), lambda j, i: (i, 0)), w_spec]
    operands = [a, w]
    if shift:
        lane_tiles = tn // LANES
        in_specs.append(pl.BlockSpec((None, LANES, kdim), lambda j, i: (layer, (cb0 + j + 1) * lane_tiles, 0)))
        operands.append(w)
    return pl.pallas_call(
        functools.partial(_proj_kernel, shift=shift, transposed=transposed),
        out_shape=jax.ShapeDtypeStruct((m, ncols), out_dtype),
        grid=(ncols // tn, m // tm),
        in_specs=in_specs,
        out_specs=pl.BlockSpec((tm, tn), lambda j, i: (i, j)),
        scratch_shapes=[pltpu.VMEM(w_block[1:], BF16)],
        compiler_params=_cparams(("parallel", "arbitrary")),
        name="projection",
    )(*operands)


def _gelu_tanh(x):
    return 0.5 * x * (1.0 + jnp.tanh(math.sqrt(2.0 / math.pi) * (x + 0.044715 * (x * x * x))))


def _compress_kernel(x_ref, pe_ref, w1_ref, w2_ref, o_ref, xf_ref):
    s = x_ref.shape[0]
    nch = s // CMP_STRIDE
    xf_ref[...] = x_ref[...].astype(F32)
    acc_lo = jnp.zeros((nch, HEAD_DIM), F32)
    acc_hi = jnp.zeros((nch, HEAD_DIM), F32)
    for l in range(CMP_STRIDE):
        xl = xf_ref[pl.ds(l, nch, stride=CMP_STRIDE), :]
        lo_in = (xl + pe_ref[l:l + 1, :]).astype(BF16)
        hi_in = (xl + pe_ref[CMP_STRIDE + l:CMP_STRIDE + l + 1, :]).astype(BF16)
        acc_lo += jnp.dot(lo_in, w1_ref[l * HEAD_DIM:(l + 1) * HEAD_DIM, :], preferred_element_type=F32)
        acc_hi += jnp.dot(hi_in, w1_ref[(CMP_STRIDE + l) * HEAD_DIM:(CMP_STRIDE + l + 1) * HEAD_DIM, :],
                          preferred_element_type=F32)
    pre = acc_lo + pltpu.roll(acc_hi, nch - 1, 0)
    hid = _gelu_tanh(pre).astype(BF16)
    o_ref[...] = jnp.dot(hid, w2_ref[...], preferred_element_type=F32).astype(o_ref.dtype)


def _nsa_compress(qkv, pe, w1, w2):
    bsz, s, _ = qkv.shape
    nch = s // CMP_STRIDE
    kv_block0 = NSA_HEADS
    return pl.pallas_call(
        _compress_kernel,
        out_shape=jax.ShapeDtypeStruct((bsz, 2, NSA_GROUPS, nch, HEAD_DIM), BF16),
        grid=(bsz, 2, NSA_GROUPS),
        in_specs=[
            pl.BlockSpec((None, s, HEAD_DIM), lambda b, w, g: (b, 0, kv_block0 + w * NSA_GROUPS + g)),
            pl.BlockSpec((None, CMP_LEN, HEAD_DIM), lambda b, w, g: (w, 0, 0)),
            pl.BlockSpec((None, CMP_LEN * HEAD_DIM, HEAD_DIM), lambda b, w, g: (w, 0, 0)),
            pl.BlockSpec((None, HEAD_DIM, HEAD_DIM), lambda b, w, g: (w, 0, 0)),
        ],
        out_specs=pl.BlockSpec((None, None, None, nch, HEAD_DIM), lambda b, w, g: (b, w, g, 0, 0)),
        scratch_shapes=[pltpu.VMEM((s, HEAD_DIM), F32)],
        compiler_params=_cparams(("parallel", "parallel", "parallel")),
        name="nsa_compress",
    )(qkv, pe.astype(F32), w1.astype(BF16), w2.astype(BF16))


def _split3(x):
    hi = x.astype(BF16)
    r1 = x - hi.astype(F32)
    mid = r1.astype(BF16)
    lo = (r1 - mid.astype(F32)).astype(BF16)
    return hi, mid, lo


def _softmax_rows(s):
    m = jnp.max(s, axis=-1, keepdims=True)
    p = jnp.exp(s - m)
    return p, jnp.sum(p, axis=-1, keepdims=True)


def _nsa_attn_kernel(slopes_ref, slx_ref, kx_ref, q_ref, kcmp_ref, vcmp_ref, ks_ref, vs_ref, kw_ref, vw_ref,
                     gl_ref, z_ref, y_ref, ksx_ref, kwx_ref, qx_ref, oc_ref, *, tq, seq):
    qi = pl.program_id(2)
    t0 = qi * tq
    scale = HEAD_DIM ** -0.5
    n_cmp_pad = kcmp_ref.shape[0]
    n_sel = seq // SEL_LEN
    rows = NSA_HPG * tq
    nt = (((1,), (1,)), ((), ()))

    @pl.when(qi == 0)
    def _():
        kx = kx_ref[...]
        lane_k = lax.broadcasted_iota(jnp.int32, kx.shape, 1)
        ksx_ref[:, 0:HEAD_DIM] = ks_ref[...]
        ksx_ref[:, HEAD_DIM:] = kx
        kwx_ref[:, 0:HEAD_DIM] = kw_ref[...]
        kwx_ref[:, HEAD_DIM:] = jnp.where(lane_k >= n_sel, kx, jnp.zeros_like(kx))

    tpos_i = t0 + lax.broadcasted_iota(jnp.int32, (tq, 1), 0)
    tpos = tpos_i.astype(F32)

    cmp_end = (lax.broadcasted_iota(jnp.int32, (1, n_cmp_pad), 1) * CMP_STRIDE + (CMP_LEN - 1)).astype(F32)
    dist_c = tpos - cmp_end
    valid_c = dist_c >= 0.0
    row_valid = tpos >= float(CMP_LEN - 1)
    kcmp = kcmp_ref[...]
    vcmp = vcmp_ref[...]
    pg = jnp.zeros((tq, n_cmp_pad), F32)
    for j in range(NSA_HPG):
        qh = q_ref[:, j * HEAD_DIM:(j + 1) * HEAD_DIM]
        sl = slopes_ref[j:j + 1, :]
        s = lax.dot_general(qh, kcmp, nt, preferred_element_type=F32) * scale - sl * dist_c
        s = jnp.where(valid_c, s, NEG_INF)
        p, l = _softmax_rows(s)
        p = jnp.where(valid_c & row_valid, p * (1.0 / l), 0.0)
        pg = pg + p
        oc_ref[j * tq:(j + 1) * tq, :] = jnp.dot(p.astype(BF16), vcmp, preferred_element_type=F32)

    n_idx = lax.broadcasted_iota(jnp.int32, (n_cmp_pad, LANES), 0)
    j_idx = lax.broadcasted_iota(jnp.int32, (n_cmp_pad, LANES), 1)
    dd = n_idx - (SEL_LEN // CMP_STRIDE) * j_idx + (CMP_LEN // CMP_STRIDE - 1)
    pool = jnp.where((dd == 0) | (dd == 4), 1.0, jnp.where((dd >= 1) & (dd <= 3), 2.0, 0.0)).astype(BF16)
    p_slc = jnp.zeros((tq, LANES), F32)
    for part in _split3(pg):
        p_slc = p_slc + jnp.dot(part, pool, preferred_element_type=F32)

    blk = lax.broadcasted_iota(jnp.int32, (1, LANES), 1)
    cur = lax.shift_right_logical(tpos_i, SEL_SHIFT)
    forced = (blk == 0) | (blk == cur) | (blk == cur - 1)
    future = blk > cur
    score = jnp.where(forced, FORCE_SCORE, jnp.where(future, -1.0, p_slc))
    score = jnp.where(blk < n_sel, score, -2.0)
    score_t = score.T[0:n_sel, :]
    blk_t = lax.broadcasted_iota(jnp.int32, (n_sel, 1), 0)
    rank = jnp.zeros((n_sel, tq), F32)
    for jp in range(n_sel):
        other = score_t[jp:jp + 1, :]
        ahead = (other > score_t) | ((other == score_t) & (blk_t > jp))
        rank = rank + jnp.where(ahead, 1.0, 0.0)
    pen_t = jnp.where(rank < float(min(SEL_TOPK, n_sel)), 0.0, -NSA_MASK_BIG)
    penalty = jnp.concatenate([pen_t, jnp.zeros((LANES - n_sel, tq), F32)], axis=0).T

    for j in range(NSA_HPG):
        qx_ref[j * tq:(j + 1) * tq, 0:HEAD_DIM] = q_ref[:, j * HEAD_DIM:(j + 1) * HEAD_DIM]
        qx_ref[j * tq:(j + 1) * tq, HEAD_DIM:] = (penalty + slx_ref[j:j + 1, :]).astype(BF16)

    t_loc = lax.broadcasted_iota(jnp.int32, (tq, 1), 0)
    k_loc = lax.broadcasted_iota(jnp.int32, (1, tq), 1)
    causal = k_loc <= t_loc
    far_ok = k_loc > t_loc
    n_back = WINDOW // tq
    exp2_scale = scale * math.log2(math.e)
    gates = _sigmoid(gl_ref[...].astype(F32))
    lane_g = lax.broadcasted_iota(jnp.int32, (1, LANES), 1)
    head0 = pl.program_id(1) * NSA_HPG

    def attend(qx, pieces):
        ss = []
        for kx_p, _, mask in pieces:
            s = lax.dot_general(qx, kx_p, nt, preferred_element_type=F32)
            ss.append(s if mask is None else jnp.where(mask, s, NEG_INF))
        m = functools.reduce(jnp.maximum, [jnp.max(s, axis=-1, keepdims=True) for s in ss])
        l = 0.0
        o = 0.0
        for s, (_, v_p, _) in zip(ss, pieces):
            p = jnp.exp2((s - m) * exp2_scale)
            l = l + jnp.sum(p, axis=-1, keepdims=True)
            o = o + jnp.dot(p.astype(BF16), v_p, preferred_element_type=F32)
        return o * (1.0 / l)

    def variant(n):
        d0 = (n - 1) * tq

        def head_body(j, carry):
            r0 = pl.multiple_of(j * tq, tq)
            c0 = pl.multiple_of(j * HEAD_DIM, HEAD_DIM)
            qx = qx_ref[pl.ds(r0, tq), :]
            sel_pieces = [(ksx_ref[d0:d0 + tq, :], vs_ref[d0:d0 + tq, :], causal)]
            if n > 1:
                sel_pieces.append((ksx_ref[0:d0, :], vs_ref[0:d0, :], None))
            o_slc = attend(qx, sel_pieces)
            win_pieces = [(kwx_ref[d0:d0 + tq, :], vw_ref[d0:d0 + tq, :], causal)]
            for w in range(1, min(n - 1, n_back) + 1):
                k0 = d0 - w * tq
                win_pieces.append((kwx_ref[k0:k0 + tq, :], vw_ref[k0:k0 + tq, :], far_ok if w == n_back else None))
            o_win = attend(qx, win_pieces)

            def gate(branch):
                pick = lane_g == branch * NSA_HEADS + head0 + j
                return jnp.sum(jnp.where(pick, gates, 0.0), axis=-1, keepdims=True)

            o = gate(0) * oc_ref[pl.ds(r0, tq), :] + gate(1) * o_slc + gate(2) * o_win
            zj = z_ref[:, pl.ds(c0, HEAD_DIM)].astype(F32)
            y_ref[:, pl.ds(c0, HEAD_DIM)] = (o * _silu(zj)).astype(y_ref.dtype)
            return carry

        lax.fori_loop(0, NSA_HPG, head_body, 0, unroll=4)

    for n in range(1, seq // tq + 1):
        pl.when(qi == n - 1)(functools.partial(variant, n))


def _alibi_slopes():
    return 2.0 ** (-ALIBI_MAX_EXP * jnp.arange(1, NSA_HEADS + 1, dtype=F32) / NSA_HEADS)


def _nsa_tables(seq):
    slopes = _alibi_slopes()
    slope_rows = jnp.broadcast_to(slopes.reshape(NSA_GROUPS, NSA_HPG, 1), (NSA_GROUPS, NSA_HPG, LANES))
    n_sel = seq // SEL_LEN
    parts = []
    rest = slopes * (HEAD_DIM ** 0.5)
    for _ in range(3):
        part = rest.astype(BF16).astype(F32)
        parts.append(part)
        rest = rest - part
    slx = jnp.zeros((NSA_HEADS, LANES), F32)
    for c, part in enumerate(parts + parts):
        slx = slx.at[:, n_sel + c].set(part)
    slx = slx.reshape(NSA_GROUPS, NSA_HPG, LANES)
    key = jnp.arange(seq, dtype=jnp.int32)
    lane = jnp.arange(LANES, dtype=jnp.int32)[None, :]
    hi = ((key // SEL_LEN) * SEL_LEN).astype(F32)[:, None]
    lo = (key % SEL_LEN).astype(F32)[:, None]
    kx = jnp.where(lane == (key // SEL_LEN)[:, None], 1.0, 0.0)
    kx = jnp.where((lane >= n_sel) & (lane < n_sel + 3), hi, kx)
    kx = jnp.where((lane >= n_sel + 3) & (lane < n_sel + 6), lo, kx)
    return slope_rows, slx, kx.astype(BF16)


def _nsa_attention(qkv, kv_cmp, gl, z, tq=256):
    bsz, s, _ = qkv.shape
    gw = NSA_HPG * HEAD_DIM
    ncp = kv_cmp.shape[3]
    kvb = NSA_HEADS
    rows = NSA_HPG * tq
    assert WINDOW % tq == 0 and s // SEL_LEN + 6 <= LANES and tq % SEL_LEN == 0
    slope_rows, slx, kx = _nsa_tables(s)

    def kv_spec(which):
        return pl.BlockSpec((None, s, HEAD_DIM), lambda b, g, i: (b, 0, kvb + which * NSA_GROUPS + g))

    return pl.pallas_call(
        functools.partial(_nsa_attn_kernel, tq=tq, seq=s),
        out_shape=jax.ShapeDtypeStruct((bsz, s, NSA_HEADS * HEAD_DIM), BF16),
        grid=(bsz, NSA_GROUPS, s // tq),
        in_specs=[
            pl.BlockSpec((None, NSA_HPG, LANES), lambda b, g, i: (g, 0, 0)),
            pl.BlockSpec((None, NSA_HPG, LANES), lambda b, g, i: (g, 0, 0)),
            pl.BlockSpec((s, LANES), lambda b, g, i: (0, 0)),
            pl.BlockSpec((None, tq, gw), lambda b, g, i: (b, i, g)),
            pl.BlockSpec((None, None, None, ncp, HEAD_DIM), lambda b, g, i: (b, 0, g, 0, 0)),
            pl.BlockSpec((None, None, None, ncp, HEAD_DIM), lambda b, g, i: (b, 1, g, 0, 0)),
            kv_spec(2), kv_spec(3), kv_spec(4), kv_spec(5),
            pl.BlockSpec((None, tq, LANES), lambda b, g, i: (b, i, 0)),
            pl.BlockSpec((None, tq, gw), lambda b, g, i: (b, i, g)),
        ],
        out_specs=pl.BlockSpec((None, tq, gw), lambda b, g, i: (b, i, g)),
        scratch_shapes=[
            pltpu.VMEM((s, 2 * HEAD_DIM), BF16),
            pltpu.VMEM((s, 2 * HEAD_DIM), BF16),
            pltpu.VMEM((rows, 2 * HEAD_DIM), BF16),
            pltpu.VMEM((rows, HEAD_DIM), F32),
        ],
        compiler_params=_cparams(("parallel", "parallel", "arbitrary")),
        name="nsa_attention",
    )(slope_rows, slx, kx, qkv, kv_cmp, kv_cmp, qkv, qkv, qkv, qkv, gl, z)


def _nsa_mixer(h2d, bsz, seq, layer, w_in, cmp_pe, cmp_w1, cmp_w2, w_out):
    inner = NSA_HEADS * HEAD_DIM
    kv = NSA_GROUPS * HEAD_DIM
    n_qkv = inner + 6 * kv
    n_gl = 3 * NSA_HEADS
    assert n_gl < LANES and n_qkv % LANES == 0
    w_in_t = jnp.swapaxes(w_in, 1, 2)
    qkv = _proj(h2d, w_in_t, BF16, 0, n_qkv, layer, transposed=True).reshape(bsz, seq, n_qkv)
    gl = _proj(h2d, w_in_t, F32, n_qkv, LANES, layer, transposed=True).reshape(bsz, seq, LANES)
    z = _proj(h2d, w_in_t, BF16, n_qkv, inner, layer, shift=n_gl, transposed=True).reshape(bsz, seq, inner)
    kv_cmp = _nsa_compress(qkv, cmp_pe[layer], cmp_w1[layer], cmp_w2[layer])
    y = _nsa_attention(qkv, kv_cmp, gl, z)
    return _proj(y.reshape(bsz * seq, inner), w_out, BF16, layer=layer)


def _rg_kernel(xb_ref, z_ref, cw_ref, cb_ref, gw_ref, gb_ref, lam_ref, y_ref, xpad_ref, h_ref, *, ts):
    si = pl.program_id(2)
    halo = SUBLANES

    @pl.when(si == 0)
    def _():
        xpad_ref[0:halo, :] = jnp.zeros((halo, xpad_ref.shape[1]), F32)
        h_ref[...] = jnp.zeros_like(h_ref)

    x = xb_ref[...]
    xpad_ref[halo:halo + ts, :] = x
    xc = cb_ref[...] + cw_ref[RG_CONV - 1:RG_CONV, :] * x
    for k in range(RG_CONV - 1):
        shift = RG_CONV - 1 - k
        xc = xc + cw_ref[k:k + 1, :] * xpad_ref[halo - shift:halo - shift + ts, :]
    xpad_ref[0:halo, :] = x[ts - halo:ts, :]

    xcb = xc.astype(BF16)
    gate_i = _sigmoid(jnp.dot(xcb, gw_ref[0], preferred_element_type=F32) + gb_ref[0])
    gate_r = _sigmoid(jnp.dot(xcb, gw_ref[1], preferred_element_type=F32) + gb_ref[1])
    nl = -lam_ref[...]
    softplus = jnp.maximum(nl, 0.0) + jnp.log(1.0 + jnp.exp(-jnp.abs(nl)))
    log_a = (-RG_C) * gate_r * softplus
    a = jnp.exp(log_a)
    mult = jnp.sqrt(1.0 - a * a)
    row = lax.broadcasted_iota(jnp.int32, (ts, 1), 0)
    mult = jnp.where((row + si * ts) == 0, 1.0, mult)
    u = mult * gate_i * xc

    d = 1
    while d < ts:
        keep = row >= d
        a_sh = pltpu.roll(a, d, 0)
        u_sh = pltpu.roll(u, d, 0)
        u = jnp.where(keep, a * u_sh + u, u)
        a = jnp.where(keep, a * a_sh, a)
        d *= 2
    hs = a * h_ref[...] + u
    h_ref[...] = hs[ts - 1:ts, :]
    y_ref[...] = (hs * _silu(z_ref[...].astype(F32))).astype(y_ref.dtype)


def _rg_core(xb, z, conv_w, conv_b, gate_w, gate_b, lam, ts=512):
    bsz, s, w = xb.shape
    cb = w // RG_BLOCKS
    return pl.pallas_call(
        functools.partial(_rg_kernel, ts=ts),
        out_shape=jax.ShapeDtypeStruct((bsz, s, w), BF16),
        grid=(bsz, RG_BLOCKS, s // ts),
        in_specs=[
            pl.BlockSpec((None, ts, cb), lambda b, n, i: (b, i, n)),
            pl.BlockSpec((None, ts, cb), lambda b, n, i: (b, i, n)),
            pl.BlockSpec((RG_CONV, cb), lambda b, n, i: (0, n)),
            pl.BlockSpec((1, cb), lambda b, n, i: (0, n)),
            pl.BlockSpec((2, None, cb, cb), lambda b, n, i: (0, n, 0, 0)),
            pl.BlockSpec((2, None, 1, cb), lambda b, n, i: (0, n, 0, 0)),
            pl.BlockSpec((1, cb), lambda b, n, i: (0, n)),
        ],
        out_specs=pl.BlockSpec((None, ts, cb), lambda b, n, i: (b, i, n)),
        scratch_shapes=[pltpu.VMEM((ts + SUBLANES, cb), F32), pltpu.VMEM((1, cb), F32)],
        compiler_params=_cparams(("parallel", "parallel", "arbitrary")),
        name="rglru_core",
    )(xb, z, conv_w.astype(F32), conv_b.reshape(1, w).astype(F32), gate_w.astype(BF16),
      gate_b.reshape(2, RG_BLOCKS, 1, cb).astype(F32), lam.reshape(1, w).astype(F32))


def _rglru_mixer(h2d, bsz, seq, w_in, conv_w, conv_b, gate_w, gate_b, lam, w_out):
    width = w_out.shape[0]
    xb = _proj(h2d, w_in, F32, 0, width).reshape(bsz, seq, width)
    z = _proj(h2d, w_in, BF16, width, width).reshape(bsz, seq, width)
    y = _rg_core(xb, z, conv_w, conv_b, gate_w, gate_b, lam)
    return _proj(y.reshape(bsz * seq, width), w_out, BF16)


def _hg_kernel(q_ref, f_ref, v_ref, g_ref, lbl_ref, ng_ref, y_ref, state_ref, b_ref, k_ref, *, tc, layer, heads):
    @pl.when(pl.program_id(2) == 0)
    def _():
        state_ref[...] = jnp.zeros_like(state_ref)

    dk = q_ref.shape[1] // heads
    dv = v_ref.shape[1] // heads
    for hh in range(heads):
        ks = slice(hh * dk, (hh + 1) * dk)
        vs = slice(hh * dv, (hh + 1) * dv)
        _hg_head(q_ref.at[:, ks], f_ref.at[:, ks], v_ref.at[:, vs], g_ref.at[:, vs], lbl_ref.at[:, ks], ng_ref,
                 y_ref.at[:, vs], state_ref.at[hh], b_ref.at[:, ks], k_ref.at[:, ks], tc=tc, layer=layer)


def _hg_head(q_ref, f_ref, v_ref, g_ref, lbl_ref, ng_ref, y_ref, state_ref, b_ref, k_ref, *, tc, layer):
    dk = q_ref.shape[1]
    nt = (((1,), (1,)), ((), ()))
    tn = (((0,), (0,)), ((), ()))

    lg = lbl_ref[...]
    e = jnp.exp(lg - jnp.max(lg, axis=0, keepdims=True))
    pl_sm = e * (1.0 / jnp.sum(e, axis=0, keepdims=True))
    lb = jnp.zeros((1, dk), F32)
    for r in range(1, layer + 1):
        lb = lb + pl_sm[r:r + 1, :]

    q = _silu(q_ref[...])
    fg = lb + (1.0 - lb) * _sigmoid(f_ref[...])
    kk = 1.0 - fg
    b = jnp.log2(fg)
    row = lax.broadcasted_iota(jnp.int32, (tc, 1), 0)
    rc = row & (HG_CHUNK - 1)
    d = 1
    while d < HG_CHUNK:
        b = b + jnp.where(rc >= d, pltpu.roll(b, d, 0), 0.0)
        d *= 2
    b_ref[...] = b
    k_ref[...] = kk

    nsub = tc // HG_SUB
    lane = lax.broadcasted_iota(jnp.int32, (dk, LANES), 1)
    acc = jnp.zeros((tc, LANES), F32)
    for s in range(HG_SUB):
        b_s = jnp.concatenate(
            [jnp.broadcast_to(b_ref[i * HG_SUB + s:i * HG_SUB + s + 1, :], (HG_SUB, dk)) for i in range(nsub)], axis=0)
        k_s = jnp.concatenate(
            [jnp.broadcast_to(k_ref[i * HG_SUB + s:i * HG_SUB + s + 1, :], (HG_SUB, dk)) for i in range(nsub)], axis=0)
        m_s = q * jnp.exp2(jnp.minimum(b - b_s, 0.0)) * k_s
        w_s = jnp.where(((lane & (HG_SUB - 1)) == s) & (lane < HG_CHUNK), 1.0, 0.0).astype(BF16)
        acc = acc + jnp.dot(m_s.astype(BF16), w_s, preferred_element_type=F32)

    col = lax.broadcasted_iota(jnp.int32, (HG_CHUNK, HG_CHUNK), 1)
    rw = lax.broadcasted_iota(jnp.int32, (HG_CHUNK, HG_CHUNK), 0)
    col_sub = lax.shift_right_logical(col, HG_SUB_SHIFT)
    rw_sub = lax.shift_right_logical(rw, HG_SUB_SHIFT)
    diag_mask = (col_sub == rw_sub) & (col <= rw)

    state = state_ref[...]
    nsc = HG_CHUNK // HG_SUB
    for c in range(tc // HG_CHUNK):
        r0 = c * HG_CHUNK
        bc = b[r0:r0 + HG_CHUNK, :]
        qc = q[r0:r0 + HG_CHUNK, :]
        kc = kk[r0:r0 + HG_CHUNK, :]
        vc = v_ref[r0:r0 + HG_CHUNK, :]
        o = lax.dot_general((qc * jnp.exp2(bc)).astype(BF16), state.astype(BF16), nt, preferred_element_type=F32)
        blocks = [jnp.zeros((HG_SUB, HG_CHUNK), F32)]
        for i in range(1, nsc):
            n_prev = i * HG_SUB
            r_i = bc[n_prev - 1:n_prev, :]
            q_i = (qc[n_prev:n_prev + HG_SUB, :] * jnp.exp2(bc[n_prev:n_prev + HG_SUB, :] - r_i)).astype(BF16)
            k_i = kc[0:n_prev, :] * jnp.exp2(r_i - bc[0:n_prev, :])
            k_i = jnp.concatenate([k_i, jnp.zeros((HG_CHUNK - n_prev, dk), F32)], axis=0).astype(BF16)
            blocks.append(lax.dot_general(q_i, k_i, nt, preferred_element_type=F32))
        att_off = jnp.concatenate(blocks, axis=0)
        att = jnp.where(diag_mask, acc[r0:r0 + HG_CHUNK, 0:HG_CHUNK],
                        jnp.where(col_sub < rw_sub, att_off, 0.0))
        o = o + jnp.dot(att.astype(BF16), vc, preferred_element_type=F32)
        b_last = bc[HG_CHUNK - 1:HG_CHUNK, :]
        k_dec = (kc * jnp.exp2(b_last - bc)).astype(BF16)
        state = state * jnp.exp2(b_last) + lax.dot_general(vc, k_dec, tn, preferred_element_type=F32)
        ms = jnp.mean(o * o, axis=-1, keepdims=True)
        on = o * lax.rsqrt(ms + NORM_EPS) * ng_ref[...]
        gc = g_ref[r0:r0 + HG_CHUNK, :].astype(F32)
        y_ref[r0:r0 + HG_CHUNK, :] = (on * _silu(gc)).astype(y_ref.dtype)
    state_ref[...] = state


def _hg_core(qf, vg, lb_logits, norm_gain, layer, tc=512, heads=2):
    bsz, s, _ = qf.shape
    dk = qf.shape[2] // (2 * HG_HEADS)
    dv = vg.shape[2] // (2 * HG_HEADS)
    nl = lb_logits.shape[0]
    groups = HG_HEADS // heads
    wk, wv = heads * dk, heads * dv
    return pl.pallas_call(
        functools.partial(_hg_kernel, tc=tc, layer=layer, heads=heads),
        out_shape=jax.ShapeDtypeStruct((bsz, s, HG_HEADS * dv), BF16),
        grid=(bsz, groups, s // tc),
        in_specs=[
            pl.BlockSpec((None, tc, wk), lambda b, h, i: (b, i, h)),
            pl.BlockSpec((None, tc, wk), lambda b, h, i: (b, i, groups + h)),
            pl.BlockSpec((None, tc, wv), lambda b, h, i: (b, i, h)),
            pl.BlockSpec((None, tc, wv), lambda b, h, i: (b, i, groups + h)),
            pl.BlockSpec((nl, wk), lambda b, h, i: (0, h)),
            pl.BlockSpec((1, dv), lambda b, h, i: (0, 0)),
        ],
        out_specs=pl.BlockSpec((None, tc, wv), lambda b, h, i: (b, i, h)),
        scratch_shapes=[pltpu.VMEM((heads, dv, dk), F32), pltpu.VMEM((tc, wk), F32), pltpu.VMEM((tc, wk), F32)],
        compiler_params=_cparams(("parallel", "parallel", "arbitrary")),
        name="hgrn2_core",
    )(qf, qf, vg, vg, lb_logits.astype(F32), norm_gain.reshape(1, dv).astype(F32))


def _hgrn2_mixer(h2d, bsz, seq, w_in, lb_logits, layer, norm_gain, w_out):
    val = w_out.shape[0]
    key = (w_in.shape[1] - 2 * val) // 2
    qf = _proj(h2d, w_in, F32, 0, 2 * key).reshape(bsz, seq, 2 * key)
    vg = _proj(h2d, w_in, BF16, 2 * key, 2 * val).reshape(bsz, seq, 2 * val)
    y = _hg_core(qf, vg, lb_logits, norm_gain, layer)
    return _proj(y.reshape(bsz * seq, val), w_out, BF16)


def kernel(x, pre_norm_gain, post_norm_gain, nsa_w_in, nsa_cmp_pe, nsa_cmp_w1, nsa_cmp_w2, nsa_w_out,
           rg_w_in, rg_conv_w, rg_conv_b, rg_gate_w, rg_gate_b, rg_lambda, rg_w_out,
           hg_w_in, hg_lb_logits, hg_norm_gain, hg_w_out):
    bsz, seq, d = x.shape
    depth = pre_norm_gain.shape[0]
    x2d = x.reshape(bsz * seq, d)
    h = _prenorm(x2d, pre_norm_gain[0])
    for i in range(depth):
        kind, j = i % 3, i // 3
        if kind == 0:
            y = _nsa_mixer(h, bsz, seq, j, nsa_w_in, nsa_cmp_pe, nsa_cmp_w1, nsa_cmp_w2, nsa_w_out)
        elif kind == 1:
            y = _rglru_mixer(h, bsz, seq, rg_w_in[j], rg_conv_w[j], rg_conv_b[j], rg_gate_w[j], rg_gate_b[j],
                             rg_lambda[j], rg_w_out[j])
        else:
            y = _hgrn2_mixer(h, bsz, seq, hg_w_in[j], hg_lb_logits, i, hg_norm_gain[j], hg_w_out[j])
        next_gain = pre_norm_gain[i + 1] if i + 1 < depth else None
        x2d, h = _postnorm_residual(x2d, y, post_norm_gain[i], next_gain)
    return x2d.reshape(bsz, seq, d)
```

```python
import functools
import math

import jax
import jax.numpy as jnp
from jax import lax
from jax.experimental import pallas as pl
from jax.experimental.pallas import tpu as pltpu

F32 = jnp.float32
BF16 = jnp.bfloat16

NORM_EPS = 1e-6
NEG_INF = -1e30
FORCE_SCORE = 1e6

LANES = 128
SUBLANES = 8
V7X_VMEM_LIMIT_BYTES = 56 * 1024 * 1024

NSA_HEADS = 32
NSA_GROUPS = 4
NSA_HPG = NSA_HEADS // NSA_GROUPS
HEAD_DIM = 128
CMP_LEN = 32
CMP_STRIDE = 16
SEL_LEN = 64
SEL_SHIFT = 6
SEL_TOPK = 16
WINDOW = 512
ALIBI_MAX_EXP = 8.0
NSA_MASK_BIG = 2.0 ** 40
NSA_STACK = 4

RG_BLOCKS = 16
RG_CONV = 4
RG_C = 8.0

HG_HEADS = 32
HG_CHUNK = 64
HG_SUB = 8
HG_SUB_SHIFT = 3


def _cparams(sem):
    return pltpu.CompilerParams(dimension_semantics=sem, vmem_limit_bytes=V7X_VMEM_LIMIT_BYTES)


def _sigmoid(x):
    return 0.5 * jnp.tanh(0.5 * x) + 0.5


def _silu(x):
    return x * _sigmoid(x)


def _prenorm_kernel(x_ref, g_ref, o_ref):
    x = x_ref[...]
    ms = jnp.mean(x * x, axis=-1, keepdims=True)
    o_ref[...] = (x * lax.rsqrt(ms + NORM_EPS) * g_ref[...]).astype(o_ref.dtype)


def _prenorm(x2d, gain, tm=256):
    t, d = x2d.shape
    return pl.pallas_call(
        _prenorm_kernel,
        out_shape=jax.ShapeDtypeStruct((t, d), BF16),
        grid=(t // tm,),
        in_specs=[pl.BlockSpec((tm, d), lambda i: (i, 0)), pl.BlockSpec((1, d), lambda i: (0, 0))],
        out_specs=pl.BlockSpec((tm, d), lambda i: (i, 0)),
        compiler_params=_cparams(("parallel",)),
        name="prenorm",
    )(x2d, gain.reshape(1, d).astype(F32))


def _postnorm_kernel(x_ref, y_ref, g_ref, *rest):
    y = y_ref[...].astype(F32)
    ms = jnp.mean(y * y, axis=-1, keepdims=True)
    xn = x_ref[...] + y * lax.rsqrt(ms + NORM_EPS) * g_ref[...]
    if len(rest) == 1:
        rest[0][...] = xn
    else:
        gn_ref, o_ref, h_ref = rest
        o_ref[...] = xn
        ms2 = jnp.mean(xn * xn, axis=-1, keepdims=True)
        h_ref[...] = (xn * lax.rsqrt(ms2 + NORM_EPS) * gn_ref[...]).astype(h_ref.dtype)


def _postnorm_residual(x2d, y2d, gain, next_gain=None, tm=256):
    t, d = x2d.shape
    row = pl.BlockSpec((tm, d), lambda i: (i, 0))
    vec = pl.BlockSpec((1, d), lambda i: (0, 0))
    operands = [x2d, y2d, gain.reshape(1, d).astype(F32)]
    in_specs = [row, row, vec]
    out_shape = jax.ShapeDtypeStruct((t, d), F32)
    out_specs = row
    if next_gain is not None:
        operands.append(next_gain.reshape(1, d).astype(F32))
        in_specs.append(vec)
        out_shape = (out_shape, jax.ShapeDtypeStruct((t, d), BF16))
        out_specs = (row, row)
    res = pl.pallas_call(
        _postnorm_kernel,
        out_shape=out_shape,
        grid=(t // tm,),
        in_specs=in_specs,
        out_specs=out_specs,
        compiler_params=_cparams(("parallel",)),
        name="postnorm_residual",
    )(*operands)
    return res if next_gain is not None else (res, None)


def _proj_kernel(a_ref, w_ref, *rest, shift, transposed):
    if shift:
        wn_ref, o_ref, wb_ref = rest
    else:
        o_ref, wb_ref = rest

    @pl.when(pl.program_id(1) == 0)
    def _():
        if shift:
            tn = wb_ref.shape[0]
            wb_ref[0:tn - shift, :] = w_ref[shift:tn, :].astype(BF16)
            wb_ref[tn - shift:tn, :] = wn_ref[0:shift, :].astype(BF16)
        else:
            wb_ref[...] = w_ref[...].astype(BF16)

    if transposed:
        acc = lax.dot_general(a_ref[...], wb_ref[...], (((1,), (1,)), ((), ())), preferred_element_type=F32)
    else:
        acc = jnp.dot(a_ref[...], wb_ref[...], preferred_element_type=F32)
    o_ref[...] = acc.astype(o_ref.dtype)


def _proj(a, w, out_dtype, col0=0, ncols=None, layer=0, shift=0, transposed=False, tm=1024, tn=512):
    if w.ndim == 2:
        w = w.reshape((1,) + w.shape)
    m, kdim = a.shape
    k_axis, n_axis = (2, 1) if transposed else (1, 2)
    ncols = w.shape[n_axis] - col0 if ncols is None else ncols
    tm, tn = min(tm, m), min(tn, ncols)
    assert m % tm == 0 and ncols % tn == 0 and col0 % tn == 0 and w.shape[k_axis] == kdim
    assert shift == 0 or (transposed and 0 < shift < LANES and shift % (2 * SUBLANES) == 0 and tn % LANES == 0)
    cb0 = col0 // tn
    if transposed:
        w_block = (None, tn, kdim)
        w_spec = pl.BlockSpec(w_block, lambda j, i: (layer, cb0 + j, 0))
    else:
        w_block = (None, kdim, tn)
        w_spec = pl.BlockSpec(w_block, lambda j, i: (layer, 0, cb0 + j))
    in_specs = [pl.BlockSpec((tm, kdim), lambda j, i: (i, 0)), w_spec]
    operands = [a, w]
    if shift:
        lane_tiles = tn // LANES
        in_specs.append(pl.BlockSpec((None, LANES, kdim), lambda j, i: (layer, (cb0 + j + 1) * lane_tiles, 0)))
        operands.append(w)
    return pl.pallas_call(
        functools.partial(_proj_kernel, shift=shift, transposed=transposed),
        out_shape=jax.ShapeDtypeStruct((m, ncols), out_dtype),
        grid=(ncols // tn, m // tm),
        in_specs=in_specs,
        out_specs=pl.BlockSpec((tm, tn), lambda j, i: (i, j)),
        scratch_shapes=[pltpu.VMEM(w_block[1:], BF16)],
        compiler_params=_cparams(("parallel", "arbitrary")),
        name="projection",
    )(*operands)


def _gelu_tanh(x):
    return 0.5 * x * (1.0 + jnp.tanh(math.sqrt(2.0 / math.pi) * (x + 0.044715 * (x * x * x))))


def _compress_kernel(x_ref, pe_ref, w1_ref, w2_ref, o_ref, xf_ref):
    s = x_ref.shape[0]
    nch = s // CMP_STRIDE
    xf_ref[...] = x_ref[...].astype(F32)
    acc_lo = jnp.zeros((nch, HEAD_DIM), F32)
    acc_hi = jnp.zeros((nch, HEAD_DIM), F32)
    for l in range(CMP_STRIDE):
        xl = xf_ref[pl.ds(l, nch, stride=CMP_STRIDE), :]
        lo_in = (xl + pe_ref[l:l + 1, :]).astype(BF16)
        hi_in = (xl + pe_ref[CMP_STRIDE + l:CMP_STRIDE + l + 1, :]).astype(BF16)
        acc_lo += jnp.dot(lo_in, w1_ref[l * HEAD_DIM:(l + 1) * HEAD_DIM, :], preferred_element_type=F32)
        acc_hi += jnp.dot(hi_in, w1_ref[(CMP_STRIDE + l) * HEAD_DIM:(CMP_STRIDE + l + 1) * HEAD_DIM, :],
                          preferred_element_type=F32)
    pre = acc_lo + pltpu.roll(acc_hi, nch - 1, 0)
    hid = _gelu_tanh(pre).astype(BF16)
    o_ref[...] = jnp.dot(hid, w2_ref[...], preferred_element_type=F32).astype(o_ref.dtype)


def _nsa_compress(qkv, pe, w1, w2):
    bsz, s, _ = qkv.shape
    nch = s // CMP_STRIDE
    kv_block0 = NSA_HEADS
    return pl.pallas_call(
        _compress_kernel,
        out_shape=jax.ShapeDtypeStruct((bsz, 2, NSA_GROUPS, nch, HEAD_DIM), BF16),
        grid=(bsz, 2, NSA_GROUPS),
        in_specs=[
            pl.BlockSpec((None, s, HEAD_DIM), lambda b, w, g: (b, 0, kv_block0 + w * NSA_GROUPS + g)),
            pl.BlockSpec((None, CMP_LEN, HEAD_DIM), lambda b, w, g: (w, 0, 0)),
            pl.BlockSpec((None, CMP_LEN * HEAD_DIM, HEAD_DIM), lambda b, w, g: (w, 0, 0)),
            pl.BlockSpec((None, HEAD_DIM, HEAD_DIM), lambda b, w, g: (w, 0, 0)),
        ],
        out_specs=pl.BlockSpec((None, None, None, nch, HEAD_DIM), lambda b, w, g: (b, w, g, 0, 0)),
        scratch_shapes=[pltpu.VMEM((s, HEAD_DIM), F32)],
        compiler_params=_cparams(("parallel", "parallel", "parallel")),
        name="nsa_compress",
    )(qkv, pe.astype(F32), w1.astype(BF16), w2.astype(BF16))


def _split3(x):
    hi = x.astype(BF16)
    r1 = x - hi.astype(F32)
    mid = r1.astype(BF16)
    lo = (r1 - mid.astype(F32)).astype(BF16)
    return hi, mid, lo


def _softmax_rows(s):
    m = jnp.max(s, axis=-1, keepdims=True)
    p = jnp.exp(s - m)
    return p, jnp.sum(p, axis=-1, keepdims=True)


def _nsa_attn_kernel(slx_ref, kx_ref, cx_ref, q_ref, kcmp_ref, vcmp_ref, ks_ref, vs_ref, kw_ref, vw_ref,
                     gl_ref, z_ref, y_ref, ksx_ref, kwx_ref, kcx_ref, qx_ref, oc_ref, *, tq, seq):
    qi = pl.program_id(2)
    t0 = qi * tq
    scale = HEAD_DIM ** -0.5
    n_cmp_pad = kcmp_ref.shape[0]
    n_sel = seq // SEL_LEN
    rows = NSA_HPG * tq
    nt = (((1,), (1,)), ((), ()))

    @pl.when(qi == 0)
    def _():
        kx = kx_ref[...]
        lane_k = lax.broadcasted_iota(jnp.int32, kx.shape, 1)
        ksx_ref[:, 0:HEAD_DIM] = ks_ref[...]
        ksx_ref[:, HEAD_DIM:] = kx
        kwx_ref[:, 0:HEAD_DIM] = kw_ref[...]
        kwx_ref[:, HEAD_DIM:] = jnp.where(lane_k >= n_sel, kx, jnp.zeros_like(kx))

        kcx_ref[:, 0:HEAD_DIM] = kcmp_ref[...]
        kcx_ref[:, HEAD_DIM:] = cx_ref[...]

    tpos_i = t0 + lax.broadcasted_iota(jnp.int32, (tq, 1), 0)
    exp2_scale = scale * math.log2(math.e)

    for j in range(NSA_HPG):
        qx_ref[j * tq:(j + 1) * tq, 0:HEAD_DIM] = q_ref[:, j * HEAD_DIM:(j + 1) * HEAD_DIM]
        qx_ref[j * tq:(j + 1) * tq, HEAD_DIM:] = jnp.broadcast_to(slx_ref[j:j + 1, :], (tq, LANES)).astype(BF16)

    t_rows = t0 + (lax.broadcasted_iota(jnp.int32, (rows, 1), 0) & (tq - 1))
    cmp_end = lax.broadcasted_iota(jnp.int32, (1, n_cmp_pad), 1) * CMP_STRIDE + (CMP_LEN - 1)
    valid_c = t_rows >= cmp_end
    s = lax.dot_general(qx_ref[...], kcx_ref[...], nt, preferred_element_type=F32)
    s = jnp.where(valid_c, s, NEG_INF)
    m = jnp.max(s, axis=-1, keepdims=True)
    p = jnp.exp2((s - m) * exp2_scale)
    l = jnp.sum(p, axis=-1, keepdims=True)
    p = jnp.where(valid_c, p * (1.0 / l), 0.0)
    oc_ref[...] = jnp.dot(p.astype(BF16), vcmp_ref[...], preferred_element_type=F32)
    pg = p[0:tq, :]
    for j in range(1, NSA_HPG):
        pg = pg + p[j * tq:(j + 1) * tq, :]

    n_idx = lax.broadcasted_iota(jnp.int32, (n_cmp_pad, LANES), 0)
    j_idx = lax.broadcasted_iota(jnp.int32, (n_cmp_pad, LANES), 1)
    dd = n_idx - (SEL_LEN // CMP_STRIDE) * j_idx + (CMP_LEN // CMP_STRIDE - 1)
    pool = jnp.where((dd == 0) | (dd == 4), 1.0, jnp.where((dd >= 1) & (dd <= 3), 2.0, 0.0)).astype(BF16)
    p_slc = jnp.zeros((tq, LANES), F32)
    for part in _split3(pg):
        p_slc = p_slc + jnp.dot(part, pool, preferred_element_type=F32)

    blk = lax.broadcasted_iota(jnp.int32, (1, LANES), 1)
    cur = lax.shift_right_logical(tpos_i, SEL_SHIFT)
    forced = (blk == 0) | (blk == cur) | (blk == cur - 1)
    future = blk > cur
    score = jnp.where(forced, FORCE_SCORE, jnp.where(future, -1.0, p_slc))
    score = jnp.where(blk < n_sel, score, -2.0)
    score_t = score.T[0:n_sel, :]
    blk_t = lax.broadcasted_iota(jnp.int32, (n_sel, 1), 0)
    rank = jnp.zeros((n_sel, tq), F32)
    for jp in range(n_sel):
        other = score_t[jp:jp + 1, :]
        ahead = (other > score_t) | ((other == score_t) & (blk_t > jp))
        rank = rank + jnp.where(ahead, 1.0, 0.0)
    pen_t = jnp.where(rank < float(min(SEL_TOPK, n_sel)), 0.0, -NSA_MASK_BIG)
    penalty = jnp.concatenate([pen_t, jnp.zeros((LANES - n_sel, tq), F32)], axis=0).T

    for j in range(NSA_HPG):
        qx_ref[j * tq:(j + 1) * tq, HEAD_DIM:] = (penalty + slx_ref[j:j + 1, :]).astype(BF16)

    hs = NSA_STACK
    t_loc = lax.broadcasted_iota(jnp.int32, (hs * tq, 1), 0) & (tq - 1)
    k_loc = lax.broadcasted_iota(jnp.int32, (1, tq), 1)
    causal = k_loc <= t_loc
    far_ok = k_loc > t_loc
    n_back = WINDOW // tq
    gates = _sigmoid(gl_ref[...].astype(F32))
    lane_g = lax.broadcasted_iota(jnp.int32, (1, LANES), 1)
    head0 = pl.program_id(1) * NSA_HPG

    def attend(qx, pieces):
        ss = []
        for kx_p, _, mask in pieces:
            s = lax.dot_general(qx, kx_p, nt, preferred_element_type=F32)
            ss.append(s if mask is None else jnp.where(mask, s, NEG_INF))
        m = functools.reduce(jnp.maximum, [jnp.max(s, axis=-1, keepdims=True) for s in ss])
        l = 0.0
        o = 0.0
        for s, (_, v_p, _) in zip(ss, pieces):
            p = jnp.exp2((s - m) * exp2_scale)
            l = l + jnp.sum(p, axis=-1, keepdims=True)
            o = o + jnp.dot(p.astype(BF16), v_p, preferred_element_type=F32)
        return o * (1.0 / l)

    def variant(n):
        d0 = (n - 1) * tq

        def head_body(jg, carry):
            r0 = pl.multiple_of(jg * (hs * tq), hs * tq)
            qx = qx_ref[pl.ds(r0, hs * tq), :]
            sel_pieces = [(ksx_ref[d0:d0 + tq, :], vs_ref[d0:d0 + tq, :], causal)]
            if n > 1:
                sel_pieces.append((ksx_ref[0:d0, :], vs_ref[0:d0, :], None))
            o_slc = attend(qx, sel_pieces)
            win_pieces = [(kwx_ref[d0:d0 + tq, :], vw_ref[d0:d0 + tq, :], causal)]
            for w in range(1, min(n - 1, n_back) + 1):
                k0 = d0 - w * tq
                win_pieces.append((kwx_ref[k0:k0 + tq, :], vw_ref[k0:k0 + tq, :], far_ok if w == n_back else None))
            o_win = attend(qx, win_pieces)

            for jj in range(hs):
                j = jg * hs + jj
                rows_j = slice(jj * tq, (jj + 1) * tq)
                c0 = pl.multiple_of(j * HEAD_DIM, HEAD_DIM)

                def gate(branch):
                    pick = lane_g == branch * NSA_HEADS + head0 + j
                    return jnp.sum(jnp.where(pick, gates, 0.0), axis=-1, keepdims=True)

                o = (gate(0) * oc_ref[pl.ds(r0 + jj * tq, tq), :] + gate(1) * o_slc[rows_j, :]
                     + gate(2) * o_win[rows_j, :])
                zj = z_ref[:, pl.ds(c0, HEAD_DIM)].astype(F32)
                y_ref[:, pl.ds(c0, HEAD_DIM)] = (o * _silu(zj)).astype(y_ref.dtype)
            return carry

        lax.fori_loop(0, NSA_HPG // hs, head_body, 0)

    for n in range(1, seq // tq + 1):
        pl.when(qi == n - 1)(functools.partial(variant, n))


def _alibi_slopes():
    return 2.0 ** (-ALIBI_MAX_EXP * jnp.arange(1, NSA_HEADS + 1, dtype=F32) / NSA_HEADS)


def _nsa_tables(seq, n_cmp_pad):
    slopes = _alibi_slopes()
    n_sel = seq // SEL_LEN
    parts = []
    rest = slopes * (HEAD_DIM ** 0.5)
    for _ in range(3):
        part = rest.astype(BF16).astype(F32)
        parts.append(part)
        rest = rest - part
    slx = jnp.zeros((NSA_HEADS, LANES), F32)
    for c, part in enumerate(parts + parts):
        slx = slx.at[:, n_sel + c].set(part)
    slx = slx.reshape(NSA_GROUPS, NSA_HPG, LANES)
    lane = jnp.arange(LANES, dtype=jnp.int32)[None, :]

    def position_lanes(pos, base):
        hi = ((pos // SEL_LEN) * SEL_LEN).astype(F32)[:, None]
        lo = (pos % SEL_LEN).astype(F32)[:, None]
        out = jnp.where((lane >= n_sel) & (lane < n_sel + 3), hi, base)
        return jnp.where((lane >= n_sel + 3) & (lane < n_sel + 6), lo, out).astype(BF16)

    key = jnp.arange(seq, dtype=jnp.int32)
    kx = position_lanes(key, jnp.where(lane == (key // SEL_LEN)[:, None], 1.0, 0.0))
    cmp_end = jnp.arange(n_cmp_pad, dtype=jnp.int32) * CMP_STRIDE + (CMP_LEN - 1)
    cx = position_lanes(cmp_end, jnp.zeros((n_cmp_pad, LANES), F32))
    return slx, kx, cx


def _nsa_attention(qkv, kv_cmp, gl, z, tq=256):
    bsz, s, _ = qkv.shape
    gw = NSA_HPG * HEAD_DIM
    ncp = kv_cmp.shape[3]
    kvb = NSA_HEADS
    rows = NSA_HPG * tq
    assert WINDOW % tq == 0 and s // SEL_LEN + 6 <= LANES and tq % SEL_LEN == 0
    slx, kx, cx = _nsa_tables(s, ncp)

    def kv_spec(which):
        return pl.BlockSpec((None, s, HEAD_DIM), lambda b, g, i: (b, 0, kvb + which * NSA_GROUPS + g))

    return pl.pallas_call(
        functools.partial(_nsa_attn_kernel, tq=tq, seq=s),
        out_shape=jax.ShapeDtypeStruct((bsz, s, NSA_HEADS * HEAD_DIM), BF16),
        grid=(bsz, NSA_GROUPS, s // tq),
        in_specs=[
            pl.BlockSpec((None, NSA_HPG, LANES), lambda b, g, i: (g, 0, 0)),
            pl.BlockSpec((s, LANES), lambda b, g, i: (0, 0)),
            pl.BlockSpec((ncp, LANES), lambda b, g, i: (0, 0)),
            pl.BlockSpec((None, tq, gw), lambda b, g, i: (b, i, g)),
            pl.BlockSpec((None, None, None, ncp, HEAD_DIM), lambda b, g, i: (b, 0, g, 0, 0)),
            pl.BlockSpec((None, None, None, ncp, HEAD_DIM), lambda b, g, i: (b, 1, g, 0, 0)),
            kv_spec(2), kv_spec(3), kv_spec(4), kv_spec(5),
            pl.BlockSpec((None, tq, LANES), lambda b, g, i: (b, i, 0)),
            pl.BlockSpec((None, tq, gw), lambda b, g, i: (b, i, g)),
        ],
        out_specs=pl.BlockSpec((None, tq, gw), lambda b, g, i: (b, i, g)),
        scratch_shapes=[
            pltpu.VMEM((s, 2 * HEAD_DIM), BF16),
            pltpu.VMEM((s, 2 * HEAD_DIM), BF16),
            pltpu.VMEM((ncp, 2 * HEAD_DIM), BF16),
            pltpu.VMEM((rows, 2 * HEAD_DIM), BF16),
            pltpu.VMEM((rows, HEAD_DIM), F32),
        ],
        compiler_params=_cparams(("parallel", "parallel", "arbitrary")),
        name="nsa_attention",
    )(slx, kx, cx, qkv, kv_cmp, kv_cmp, qkv, qkv, qkv, qkv, gl, z)


def _nsa_mixer(h2d, bsz, seq, layer, w_in, cmp_pe, cmp_w1, cmp_w2, w_out):
    inner = NSA_HEADS * HEAD_DIM
    kv = NSA_GROUPS * HEAD_DIM
    n_qkv = inner + 6 * kv
    n_gl = 3 * NSA_HEADS
    assert n_gl < LANES and n_qkv % LANES == 0
    w_in_t = jnp.swapaxes(w_in, 1, 2)
    qkv = _proj(h2d, w_in_t, BF16, 0, n_qkv, layer, transposed=True).reshape(bsz, seq, n_qkv)
    gl = _proj(h2d, w_in_t, F32, n_qkv, LANES, layer, transposed=True).reshape(bsz, seq, LANES)
    z = _proj(h2d, w_in_t, BF16, n_qkv, inner, layer, shift=n_gl, transposed=True).reshape(bsz, seq, inner)
    kv_cmp = _nsa_compress(qkv, cmp_pe[layer], cmp_w1[layer], cmp_w2[layer])
    y = _nsa_attention(qkv, kv_cmp, gl, z)
    return _proj(y.reshape(bsz * seq, inner), w_out, BF16, layer=layer)


def _rg_kernel(xb_ref, z_ref, cw_ref, cb_ref, gw_ref, gb_ref, lam_ref, y_ref, xpad_ref, h_ref, *, ts):
    si = pl.program_id(2)
    halo = SUBLANES

    @pl.when(si == 0)
    def _():
        xpad_ref[0:halo, :] = jnp.zeros((halo, xpad_ref.shape[1]), F32)
        h_ref[...] = jnp.zeros_like(h_ref)

    x = xb_ref[...]
    xpad_ref[halo:halo + ts, :] = x
    xc = cb_ref[...] + cw_ref[RG_CONV - 1:RG_CONV, :] * x
    for k in range(RG_CONV - 1):
        shift = RG_CONV - 1 - k
        xc = xc + cw_ref[k:k + 1, :] * xpad_ref[halo - shift:halo - shift + ts, :]
    xpad_ref[0:halo, :] = x[ts - halo:ts, :]

    xcb = xc.astype(BF16)
    gate_i = _sigmoid(jnp.dot(xcb, gw_ref[0], preferred_element_type=F32) + gb_ref[0])
    gate_r = _sigmoid(jnp.dot(xcb, gw_ref[1], preferred_element_type=F32) + gb_ref[1])
    nl = -lam_ref[...]
    softplus = jnp.maximum(nl, 0.0) + jnp.log(1.0 + jnp.exp(-jnp.abs(nl)))
    log_a = (-RG_C) * gate_r * softplus
    a = jnp.exp(log_a)
    one_m_a2 = 1.0 - a * a
    mult = jnp.where(one_m_a2 > 0.0, one_m_a2 * lax.rsqrt(one_m_a2), 0.0)
    row = lax.broadcasted_iota(jnp.int32, (ts, 1), 0)
    mult = jnp.where((row + si * ts) == 0, 1.0, mult)
    u = mult * gate_i * xc

    row_in_group = row & (SUBLANES - 1)
    d = 1
    while d < SUBLANES:
        keep = row_in_group >= d
        a_sh = pltpu.roll(a, d, 0)
        u_sh = pltpu.roll(u, d, 0)
        u = jnp.where(keep, a * u_sh + u, u)
        a = jnp.where(keep, a * a_sh, a)
        d *= 2
    carry = h_ref[...]
    groups = []
    for g in range(ts // SUBLANES):
        rows_g = slice(g * SUBLANES, (g + 1) * SUBLANES)
        h_g = a[rows_g, :] * carry + u[rows_g, :]
        groups.append(h_g)
        carry = h_g[SUBLANES - 1:SUBLANES, :]
    h_ref[...] = carry
    hs = jnp.concatenate(groups, axis=0)
    y_ref[...] = (hs * _silu(z_ref[...].astype(F32))).astype(y_ref.dtype)


def _rg_core(xb, z, conv_w, conv_b, gate_w, gate_b, lam, ts=512):
    bsz, s, w = xb.shape
    cb = w // RG_BLOCKS
    return pl.pallas_call(
        functools.partial(_rg_kernel, ts=ts),
        out_shape=jax.ShapeDtypeStruct((bsz, s, w), BF16),
        grid=(bsz, RG_BLOCKS, s // ts),
        in_specs=[
            pl.BlockSpec((None, ts, cb), lambda b, n, i: (b, i, n)),
            pl.BlockSpec((None, ts, cb), lambda b, n, i: (b, i, n)),
            pl.BlockSpec((RG_CONV, cb), lambda b, n, i: (0, n)),
            pl.BlockSpec((1, cb), lambda b, n, i: (0, n)),
            pl.BlockSpec((2, None, cb, cb), lambda b, n, i: (0, n, 0, 0)),
            pl.BlockSpec((2, None, 1, cb), lambda b, n, i: (0, n, 0, 0)),
            pl.BlockSpec((1, cb), lambda b, n, i: (0, n)),
        ],
        out_specs=pl.BlockSpec((None, ts, cb), lambda b, n, i: (b, i, n)),
        scratch_shapes=[pltpu.VMEM((ts + SUBLANES, cb), F32), pltpu.VMEM((1, cb), F32)],
        compiler_params=_cparams(("parallel", "parallel", "arbitrary")),
        name="rglru_core",
    )(xb, z, conv_w.astype(F32), conv_b.reshape(1, w).astype(F32), gate_w.astype(BF16),
      gate_b.reshape(2, RG_BLOCKS, 1, cb).astype(F32), lam.reshape(1, w).astype(F32))


def _rglru_mixer(h2d, bsz, seq, w_in, conv_w, conv_b, gate_w, gate_b, lam, w_out):
    width = w_out.shape[0]
    xb = _proj(h2d, w_in, F32, 0, width).reshape(bsz, seq, width)
    z = _proj(h2d, w_in, BF16, width, width).reshape(bsz, seq, width)
    y = _rg_core(xb, z, conv_w, conv_b, gate_w, gate_b, lam)
    return _proj(y.reshape(bsz * seq, width), w_out, BF16)


def _hg_kernel(q_ref, f_ref, v_ref, g_ref, lbl_ref, ng_ref, y_ref, state_ref, b_ref, k_ref, *, tc, layer, heads):
    @pl.when(pl.program_id(2) == 0)
    def _():
        state_ref[...] = jnp.zeros_like(state_ref)

    dk = q_ref.shape[1] // heads
    dv = v_ref.shape[1] // heads
    for hh in range(heads):
        ks = slice(hh * dk, (hh + 1) * dk)
        vs = slice(hh * dv, (hh + 1) * dv)
        _hg_head(q_ref.at[:, ks], f_ref.at[:, ks], v_ref.at[:, vs], g_ref.at[:, vs], lbl_ref.at[:, ks], ng_ref,
                 y_ref.at[:, vs], state_ref.at[hh], b_ref.at[:, ks], k_ref.at[:, ks], tc=tc, layer=layer)


def _hg_head(q_ref, f_ref, v_ref, g_ref, lbl_ref, ng_ref, y_ref, state_ref, b_ref, k_ref, *, tc, layer):
    dk = q_ref.shape[1]
    nt = (((1,), (1,)), ((), ()))
    tn = (((0,), (0,)), ((), ()))

    lg = lbl_ref[...]
    e = jnp.exp(lg - jnp.max(lg, axis=0, keepdims=True))
    pl_sm = e * (1.0 / jnp.sum(e, axis=0, keepdims=True))
    lb = jnp.zeros((1, dk), F32)
    for r in range(1, layer + 1):
        lb = lb + pl_sm[r:r + 1, :]

    q = _silu(q_ref[...])
    fg = lb + (1.0 - lb) * _sigmoid(f_ref[...])
    kk = 1.0 - fg
    b = jnp.log2(fg)
    row = lax.broadcasted_iota(jnp.int32, (tc, 1), 0)
    rc = row & (HG_CHUNK - 1)
    d = 1
    while d < HG_CHUNK:
        b = b + jnp.where(rc >= d, pltpu.roll(b, d, 0), 0.0)
        d *= 2
    b_ref[...] = b
    k_ref[...] = kk

    nsub = tc // HG_SUB
    lane = lax.broadcasted_iota(jnp.int32, (dk, LANES), 1)
    acc = jnp.zeros((tc, LANES), F32)
    for s in range(HG_SUB):
        b_s = jnp.concatenate(
            [jnp.broadcast_to(b_ref[i * HG_SUB + s:i * HG_SUB + s + 1, :], (HG_SUB, dk)) for i in range(nsub)], axis=0)
        k_s = jnp.concatenate(
            [jnp.broadcast_to(k_ref[i * HG_SUB + s:i * HG_SUB + s + 1, :], (HG_SUB, dk)) for i in range(nsub)], axis=0)
        m_s = q * jnp.exp2(jnp.minimum(b - b_s, 0.0)) * k_s
        w_s = jnp.where(((lane & (HG_SUB - 1)) == s) & (lane < HG_CHUNK), 1.0, 0.0).astype(BF16)
        acc = acc + jnp.dot(m_s.astype(BF16), w_s, preferred_element_type=F32)

    col = lax.broadcasted_iota(jnp.int32, (HG_CHUNK, HG_CHUNK), 1)
    rw = lax.broadcasted_iota(jnp.int32, (HG_CHUNK, HG_CHUNK), 0)
    col_sub = lax.shift_right_logical(col, HG_SUB_SHIFT)
    rw_sub = lax.shift_right_logical(rw, HG_SUB_SHIFT)
    diag_mask = (col_sub == rw_sub) & (col <= rw)

    state = state_ref[...]
    nsc = HG_CHUNK // HG_SUB
    for c in range(tc // HG_CHUNK):
        r0 = c * HG_CHUNK
        bc = b[r0:r0 + HG_CHUNK, :]
        qc = q[r0:r0 + HG_CHUNK, :]
        kc = kk[r0:r0 + HG_CHUNK, :]
        vc = v_ref[r0:r0 + HG_CHUNK, :]
        o = lax.dot_general((qc * jnp.exp2(bc)).astype(BF16), state.astype(BF16), nt, preferred_element_type=F32)
        blocks = [jnp.zeros((HG_SUB, HG_CHUNK), F32)]
        for i in range(1, nsc):
            n_prev = i * HG_SUB
            r_i = bc[n_prev - 1:n_prev, :]
            q_i = (qc[n_prev:n_prev + HG_SUB, :] * jnp.exp2(bc[n_prev:n_prev + HG_SUB, :] - r_i)).astype(BF16)
            k_i = kc[0:n_prev, :] * jnp.exp2(r_i - bc[0:n_prev, :])
            k_i = jnp.concatenate([k_i, jnp.zeros((HG_CHUNK - n_prev, dk), F32)], axis=0).astype(BF16)
            blocks.append(lax.dot_general(q_i, k_i, nt, preferred_element_type=F32))
        att_off = jnp.concatenate(blocks, axis=0)
        att = jnp.where(diag_mask, acc[r0:r0 + HG_CHUNK, 0:HG_CHUNK],
                        jnp.where(col_sub < rw_sub, att_off, 0.0))
        o = o + jnp.dot(att.astype(BF16), vc, preferred_element_type=F32)
        b_last = bc[HG_CHUNK - 1:HG_CHUNK, :]
        k_dec = (kc * jnp.exp2(b_last - bc)).astype(BF16)
        state = state * jnp.exp2(b_last) + lax.dot_general(vc, k_dec, tn, preferred_element_type=F32)
        ms = jnp.mean(o * o, axis=-1, keepdims=True)
        on = o * lax.rsqrt(ms + NORM_EPS) * ng_ref[...]
        gc = g_ref[r0:r0 + HG_CHUNK, :].astype(F32)
        y_ref[r0:r0 + HG_CHUNK, :] = (on * _silu(gc)).astype(y_ref.dtype)
    state_ref[...] = state


def _hg_core(qf, vg, lb_logits, norm_gain, layer, tc=512, heads=2):
    bsz, s, _ = qf.shape
    dk = qf.shape[2] // (2 * HG_HEADS)
    dv = vg.shape[2] // (2 * HG_HEADS)
    nl = lb_logits.shape[0]
    groups = HG_HEADS // heads
    wk, wv = heads * dk, heads * dv
    return pl.pallas_call(
        functools.partial(_hg_kernel, tc=tc, layer=layer, heads=heads),
        out_shape=jax.ShapeDtypeStruct((bsz, s, HG_HEADS * dv), BF16),
        grid=(bsz, groups, s // tc),
        in_specs=[
            pl.BlockSpec((None, tc, wk), lambda b, h, i: (b, i, h)),
            pl.BlockSpec((None, tc, wk), lambda b, h, i: (b, i, groups + h)),
            pl.BlockSpec((None, tc, wv), lambda b, h, i: (b, i, h)),
            pl.BlockSpec((None, tc, wv), lambda b, h, i: (b, i, groups + h)),
            pl.BlockSpec((nl, wk), lambda b, h, i: (0, h)),
            pl.BlockSpec((1, dv), lambda b, h, i: (0, 0)),
        ],
        out_specs=pl.BlockSpec((None, tc, wv), lambda b, h, i: (b, i, h)),
        scratch_shapes=[pltpu.VMEM((heads, dv, dk), F32), pltpu.VMEM((tc, wk), F32), pltpu.VMEM((tc, wk), F32)],
        compiler_params=_cparams(("parallel", "parallel", "arbitrary")),
        name="hgrn2_core",
    )(qf, qf, vg, vg, lb_logits.astype(F32), norm_gain.reshape(1, dv).astype(F32))


def _hgrn2_mixer(h2d, bsz, seq, w_in, lb_logits, layer, norm_gain, w_out):
    val = w_out.shape[0]
    key = (w_in.shape[1] - 2 * val) // 2
    qf = _proj(h2d, w_in, F32, 0, 2 * key).reshape(bsz, seq, 2 * key)
    vg = _proj(h2d, w_in, BF16, 2 * key, 2 * val).reshape(bsz, seq, 2 * val)
    y = _hg_core(qf, vg, lb_logits, norm_gain, layer)
    return _proj(y.reshape(bsz * seq, val), w_out, BF16)


def kernel(x, pre_norm_gain, post_norm_gain, nsa_w_in, nsa_cmp_pe, nsa_cmp_w1, nsa_cmp_w2, nsa_w_out,
           rg_w_in, rg_conv_w, rg_conv_b, rg_gate_w, rg_gate_b, rg_lambda, rg_w_out,
           hg_w_in, hg_lb_logits, hg_norm_gain, hg_w_out):
    bsz, seq, d = x.shape
    depth = pre_norm_gain.shape[0]
    x2d = x.reshape(bsz * seq, d)
    h = _prenorm(x2d, pre_norm_gain[0])
    for i in range(depth):
        kind, j = i % 3, i // 3
        if kind == 0:
            y = _nsa_mixer(h, bsz, seq, j, nsa_w_in, nsa_cmp_pe, nsa_cmp_w1, nsa_cmp_w2, nsa_w_out)
        elif kind == 1:
            y = _rglru_mixer(h, bsz, seq, rg_w_in[j], rg_conv_w[j], rg_conv_b[j], rg_gate_w[j], rg_gate_b[j],
                             rg_lambda[j], rg_w_out[j])
        else:
            y = _hgrn2_mixer(h, bsz, seq, hg_w_in[j], hg_lb_logits, i, hg_norm_gain[j], hg_w_out[j])
        next_gain = pre_norm_gain[i + 1] if i + 1 < depth else None
        x2d, h = _postnorm_residual(x2d, y, post_norm_gain[i], next_gain)
    return x2d.reshape(bsz, seq, d)
```

```python
import functools
import math

import jax
import jax.numpy as jnp
from jax import lax
from jax.experimental import pallas as pl
from jax.experimental.pallas import tpu as pltpu

F32 = jnp.float32
BF16 = jnp.bfloat16

NORM_EPS = 1e-6
NEG_INF = -1e30
FORCE_SCORE = 1e6

LANES = 128
SUBLANES = 8
V7X_VMEM_LIMIT_BYTES = 56 * 1024 * 1024

NSA_HEADS = 32
NSA_GROUPS = 4
NSA_HPG = NSA_HEADS // NSA_GROUPS
HEAD_DIM = 128
CMP_LEN = 32
CMP_STRIDE = 16
SEL_LEN = 64
SEL_SHIFT = 6
SEL_TOPK = 16
WINDOW = 512
ALIBI_MAX_EXP = 8.0
NSA_MASK_BIG = 2.0 ** 40
NSA_STACK = 4

RG_BLOCKS = 16
RG_CONV = 4
RG_C = 8.0

HG_HEADS = 32
HG_CHUNK = 64
HG_SUB = 8
HG_SUB_SHIFT = 3


NORM_ROWS = 256
PROJ_ROWS = 1024
PROJ_COLS = 512
NSA_QUERY_TILE = 256
RG_TILE = 1024
HG_TILE = 512
HG_HEADS_PER_STEP = 2


def _cparams(sem):
    return pltpu.CompilerParams(dimension_semantics=sem, vmem_limit_bytes=V7X_VMEM_LIMIT_BYTES)


def _sigmoid(x):
    return 0.5 * jnp.tanh(0.5 * x) + 0.5


def _silu(x):
    return x * _sigmoid(x)


def _prenorm_kernel(x_ref, g_ref, o_ref):
    x = x_ref[...]
    ms = jnp.mean(x * x, axis=-1, keepdims=True)
    o_ref[...] = (x * lax.rsqrt(ms + NORM_EPS) * g_ref[...]).astype(o_ref.dtype)


def _prenorm(x2d, gain, tm=NORM_ROWS):
    t, d = x2d.shape
    return pl.pallas_call(
        _prenorm_kernel,
        out_shape=jax.ShapeDtypeStruct((t, d), BF16),
        grid=(t // tm,),
        in_specs=[pl.BlockSpec((tm, d), lambda i: (i, 0)), pl.BlockSpec((1, d), lambda i: (0, 0))],
        out_specs=pl.BlockSpec((tm, d), lambda i: (i, 0)),
        compiler_params=_cparams(("parallel",)),
        name="prenorm",
    )(x2d, gain.reshape(1, d).astype(F32))


def _postnorm_kernel(x_ref, y_ref, g_ref, *rest):
    y = y_ref[...].astype(F32)
    ms = jnp.mean(y * y, axis=-1, keepdims=True)
    xn = x_ref[...] + y * lax.rsqrt(ms + NORM_EPS) * g_ref[...]
    if len(rest) == 1:
        rest[0][...] = xn
    else:
        gn_ref, o_ref, h_ref = rest
        o_ref[...] = xn
        ms2 = jnp.mean(xn * xn, axis=-1, keepdims=True)
        h_ref[...] = (xn * lax.rsqrt(ms2 + NORM_EPS) * gn_ref[...]).astype(h_ref.dtype)


def _postnorm_residual(x2d, y2d, gain, next_gain=None, tm=NORM_ROWS):
    t, d = x2d.shape
    row = pl.BlockSpec((tm, d), lambda i: (i, 0))
    vec = pl.BlockSpec((1, d), lambda i: (0, 0))
    operands = [x2d, y2d, gain.reshape(1, d).astype(F32)]
    in_specs = [row, row, vec]
    out_shape = jax.ShapeDtypeStruct((t, d), F32)
    out_specs = row
    if next_gain is not None:
        operands.append(next_gain.reshape(1, d).astype(F32))
        in_specs.append(vec)
        out_shape = (out_shape, jax.ShapeDtypeStruct((t, d), BF16))
        out_specs = (row, row)
    res = pl.pallas_call(
        _postnorm_kernel,
        out_shape=out_shape,
        grid=(t // tm,),
        in_specs=in_specs,
        out_specs=out_specs,
        compiler_params=_cparams(("parallel",)),
        name="postnorm_residual",
    )(*operands)
    return res if next_gain is not None else (res, None)


def _proj_kernel(a_ref, w_ref, *rest, shift, transposed):
    if shift:
        wn_ref, o_ref, wb_ref = rest
    else:
        o_ref, wb_ref = rest

    @pl.when(pl.program_id(1) == 0)
    def _():
        if shift:
            tn = wb_ref.shape[0]
            wb_ref[0:tn - shift, :] = w_ref[shift:tn, :].astype(BF16)
            wb_ref[tn - shift:tn, :] = wn_ref[0:shift, :].astype(BF16)
        else:
            wb_ref[...] = w_ref[...].astype(BF16)

    if transposed:
        acc = lax.dot_general(a_ref[...], wb_ref[...], (((1,), (1,)), ((), ())), preferred_element_type=F32)
    else:
        acc = jnp.dot(a_ref[...], wb_ref[...], preferred_element_type=F32)
    o_ref[...] = acc.astype(o_ref.dtype)


def _proj(a, w, out_dtype, col0=0, ncols=None, layer=0, shift=0, transposed=False, tm=PROJ_ROWS, tn=PROJ_COLS):
    if w.ndim == 2:
        w = w.reshape((1,) + w.shape)
    m, kdim = a.shape
    k_axis, n_axis = (2, 1) if transposed else (1, 2)
    ncols = w.shape[n_axis] - col0 if ncols is None else ncols
    tm, tn = min(tm, m), min(tn, ncols)
    assert m % tm == 0 and ncols % tn == 0 and col0 % tn == 0 and w.shape[k_axis] == kdim
    assert shift == 0 or (transposed and 0 < shift < LANES and shift % (2 * SUBLANES) == 0 and tn % LANES == 0)
    cb0 = col0 // tn
    if transposed:
        w_block = (None, tn, kdim)
        w_spec = pl.BlockSpec(w_block, lambda j, i: (layer, cb0 + j, 0))
    else:
        w_block = (None, kdim, tn)
        w_spec = pl.BlockSpec(w_block, lambda j, i: (layer, 0, cb0 + j))
    in_specs = [pl.BlockSpec((tm, kdim), lambda j, i: (i, 0)), w_spec]
    operands = [a, w]
    if shift:
        lane_tiles = tn // LANES
        in_specs.append(pl.BlockSpec((None, LANES, kdim), lambda j, i: (layer, (cb0 + j + 1) * lane_tiles, 0)))
        operands.append(w)
    return pl.pallas_call(
        functools.partial(_proj_kernel, shift=shift, transposed=transposed),
        out_shape=jax.ShapeDtypeStruct((m, ncols), out_dtype),
        grid=(ncols // tn, m // tm),
        in_specs=in_specs,
        out_specs=pl.BlockSpec((tm, tn), lambda j, i: (i, j)),
        scratch_shapes=[pltpu.VMEM(w_block[1:], BF16)],
        compiler_params=_cparams(("parallel", "arbitrary")),
        name="projection",
    )(*operands)


def _gelu_tanh(x):
    return 0.5 * x * (1.0 + jnp.tanh(math.sqrt(2.0 / math.pi) * (x + 0.044715 * (x * x * x))))


def _compress_kernel(x_ref, pe_ref, w1_ref, w2_ref, o_ref, xf_ref):
    s = x_ref.shape[0]
    nch = s // CMP_STRIDE
    xf_ref[...] = x_ref[...].astype(F32)
    acc_lo = jnp.zeros((nch, HEAD_DIM), F32)
    acc_hi = jnp.zeros((nch, HEAD_DIM), F32)
    for l in range(CMP_STRIDE):
        xl = xf_ref[pl.ds(l, nch, stride=CMP_STRIDE), :]
        lo_in = (xl + pe_ref[l:l + 1, :]).astype(BF16)
        hi_in = (xl + pe_ref[CMP_STRIDE + l:CMP_STRIDE + l + 1, :]).astype(BF16)
        acc_lo += jnp.dot(lo_in, w1_ref[l * HEAD_DIM:(l + 1) * HEAD_DIM, :], preferred_element_type=F32)
        acc_hi += jnp.dot(hi_in, w1_ref[(CMP_STRIDE + l) * HEAD_DIM:(CMP_STRIDE + l + 1) * HEAD_DIM, :],
                          preferred_element_type=F32)
    pre = acc_lo + pltpu.roll(acc_hi, nch - 1, 0)
    hid = _gelu_tanh(pre).astype(BF16)
    o_ref[...] = jnp.dot(hid, w2_ref[...], preferred_element_type=F32).astype(o_ref.dtype)


def _nsa_compress(qkv, pe, w1, w2):
    bsz, s, _ = qkv.shape
    nch = s // CMP_STRIDE
    kv_block0 = NSA_HEADS
    return pl.pallas_call(
        _compress_kernel,
        out_shape=jax.ShapeDtypeStruct((bsz, 2, NSA_GROUPS, nch, HEAD_DIM), BF16),
        grid=(bsz, 2, NSA_GROUPS),
        in_specs=[
            pl.BlockSpec((None, s, HEAD_DIM), lambda b, w, g: (b, 0, kv_block0 + w * NSA_GROUPS + g)),
            pl.BlockSpec((None, CMP_LEN, HEAD_DIM), lambda b, w, g: (w, 0, 0)),
            pl.BlockSpec((None, CMP_LEN * HEAD_DIM, HEAD_DIM), lambda b, w, g: (w, 0, 0)),
            pl.BlockSpec((None, HEAD_DIM, HEAD_DIM), lambda b, w, g: (w, 0, 0)),
        ],
        out_specs=pl.BlockSpec((None, None, None, nch, HEAD_DIM), lambda b, w, g: (b, w, g, 0, 0)),
        scratch_shapes=[pltpu.VMEM((s, HEAD_DIM), F32)],
        compiler_params=_cparams(("parallel", "parallel", "parallel")),
        name="nsa_compress",
    )(qkv, pe.astype(F32), w1.astype(BF16), w2.astype(BF16))


def _split3(x):
    hi = x.astype(BF16)
    r1 = x - hi.astype(F32)
    mid = r1.astype(BF16)
    lo = (r1 - mid.astype(F32)).astype(BF16)
    return hi, mid, lo


def _nsa_attn_kernel(slx_ref, kx_ref, cx_ref, q_ref, kcmp_ref, vcmp_ref, ks_ref, vs_ref, kw_ref, vw_ref,
                     gl_ref, z_ref, y_ref, ksx_ref, kwx_ref, kcx_ref, qx_ref, oc_ref, *, tq, seq):
    qi = pl.program_id(2)
    t0 = qi * tq
    scale = HEAD_DIM ** -0.5
    n_cmp_pad = kcmp_ref.shape[0]
    n_sel = seq // SEL_LEN
    rows = NSA_HPG * tq
    nt = (((1,), (1,)), ((), ()))

    @pl.when(qi == 0)
    def _():
        kx = kx_ref[...]
        lane_k = lax.broadcasted_iota(jnp.int32, kx.shape, 1)
        ksx_ref[:, 0:HEAD_DIM] = ks_ref[...]
        ksx_ref[:, HEAD_DIM:] = kx
        kwx_ref[:, 0:HEAD_DIM] = kw_ref[...]
        kwx_ref[:, HEAD_DIM:] = jnp.where(lane_k >= n_sel, kx, jnp.zeros_like(kx))

        kcx_ref[:, 0:HEAD_DIM] = kcmp_ref[...]
        kcx_ref[:, HEAD_DIM:] = cx_ref[...]

    tpos_i = t0 + lax.broadcasted_iota(jnp.int32, (tq, 1), 0)
    exp2_scale = scale * math.log2(math.e)

    for j in range(NSA_HPG):
        qx_ref[j * tq:(j + 1) * tq, 0:HEAD_DIM] = q_ref[:, j * HEAD_DIM:(j + 1) * HEAD_DIM]
        qx_ref[j * tq:(j + 1) * tq, HEAD_DIM:] = jnp.broadcast_to(slx_ref[j:j + 1, :], (tq, LANES)).astype(BF16)

    t_rows = t0 + (lax.broadcasted_iota(jnp.int32, (rows, 1), 0) & (tq - 1))
    cmp_end = lax.broadcasted_iota(jnp.int32, (1, n_cmp_pad), 1) * CMP_STRIDE + (CMP_LEN - 1)
    valid_c = t_rows >= cmp_end
    s = lax.dot_general(qx_ref[...], kcx_ref[...], nt, preferred_element_type=F32)
    s = jnp.where(valid_c, s, NEG_INF)
    m = jnp.max(s, axis=-1, keepdims=True)
    p = jnp.exp2((s - m) * exp2_scale)
    l = jnp.sum(p, axis=-1, keepdims=True)
    p = jnp.where(valid_c, p * (1.0 / l), 0.0)
    oc_ref[...] = jnp.dot(p.astype(BF16), vcmp_ref[...], preferred_element_type=F32)
    pg = p[0:tq, :]
    for j in range(1, NSA_HPG):
        pg = pg + p[j * tq:(j + 1) * tq, :]

    n_idx = lax.broadcasted_iota(jnp.int32, (n_cmp_pad, LANES), 0)
    j_idx = lax.broadcasted_iota(jnp.int32, (n_cmp_pad, LANES), 1)
    dd = n_idx - (SEL_LEN // CMP_STRIDE) * j_idx + (CMP_LEN // CMP_STRIDE - 1)
    pool = jnp.where((dd == 0) | (dd == 4), 1.0, jnp.where((dd >= 1) & (dd <= 3), 2.0, 0.0)).astype(BF16)
    p_slc = jnp.zeros((tq, LANES), F32)
    for part in _split3(pg):
        p_slc = p_slc + jnp.dot(part, pool, preferred_element_type=F32)

    blk = lax.broadcasted_iota(jnp.int32, (1, LANES), 1)
    cur = lax.shift_right_logical(tpos_i, SEL_SHIFT)
    forced = (blk == 0) | (blk == cur) | (blk == cur - 1)
    future = blk > cur
    score = jnp.where(forced, FORCE_SCORE, jnp.where(future, -1.0, p_slc))
    score = jnp.where(blk < n_sel, score, -2.0)
    score_t = score.T[0:n_sel, :]
    blk_t = lax.broadcasted_iota(jnp.int32, (n_sel, 1), 0)
    rank = jnp.zeros((n_sel, tq), F32)
    for jp in range(n_sel):
        other = score_t[jp:jp + 1, :]
        ahead = (other > score_t) | ((other == score_t) & (blk_t > jp))
        rank = rank + jnp.where(ahead, 1.0, 0.0)
    pen_t = jnp.where(rank < float(min(SEL_TOPK, n_sel)), 0.0, -NSA_MASK_BIG)
    penalty = jnp.concatenate([pen_t, jnp.zeros((LANES - n_sel, tq), F32)], axis=0).T

    for j in range(NSA_HPG):
        qx_ref[j * tq:(j + 1) * tq, HEAD_DIM:] = (penalty + slx_ref[j:j + 1, :]).astype(BF16)

    hs = NSA_STACK
    t_loc = lax.broadcasted_iota(jnp.int32, (hs * tq, 1), 0) & (tq - 1)
    k_loc = lax.broadcasted_iota(jnp.int32, (1, tq), 1)
    causal = k_loc <= t_loc
    far_ok = k_loc > t_loc
    n_back = WINDOW // tq
    gates = _sigmoid(gl_ref[...].astype(F32))
    lane_g = lax.broadcasted_iota(jnp.int32, (1, LANES), 1)
    head0 = pl.program_id(1) * NSA_HPG

    def attend(qx, pieces):
        ss = []
        for kx_p, _, mask in pieces:
            s = lax.dot_general(qx, kx_p, nt, preferred_element_type=F32)
            ss.append(s if mask is None else jnp.where(mask, s, NEG_INF))
        m = functools.reduce(jnp.maximum, [jnp.max(s, axis=-1, keepdims=True) for s in ss])
        l = 0.0
        o = 0.0
        for s, (_, v_p, _) in zip(ss, pieces):
            p = jnp.exp2((s - m) * exp2_scale)
            l = l + jnp.sum(p, axis=-1, keepdims=True)
            o = o + jnp.dot(p.astype(BF16), v_p, preferred_element_type=F32)
        return o * (1.0 / l)

    def variant(n):
        d0 = (n - 1) * tq

        def head_body(jg, carry):
            r0 = pl.multiple_of(jg * (hs * tq), hs * tq)
            qx = qx_ref[pl.ds(r0, hs * tq), :]
            sel_pieces = [(ksx_ref[d0:d0 + tq, :], vs_ref[d0:d0 + tq, :], causal)]
            if n > 1:
                sel_pieces.append((ksx_ref[0:d0, :], vs_ref[0:d0, :], None))
            o_slc = attend(qx, sel_pieces)
            win_pieces = [(kwx_ref[d0:d0 + tq, :], vw_ref[d0:d0 + tq, :], causal)]
            for w in range(1, min(n - 1, n_back) + 1):
                k0 = d0 - w * tq
                win_pieces.append((kwx_ref[k0:k0 + tq, :], vw_ref[k0:k0 + tq, :], far_ok if w == n_back else None))
            o_win = attend(qx, win_pieces)

            for jj in range(hs):
                j = jg * hs + jj
                rows_j = slice(jj * tq, (jj + 1) * tq)
                c0 = pl.multiple_of(j * HEAD_DIM, HEAD_DIM)

                def gate(branch):
                    pick = lane_g == branch * NSA_HEADS + head0 + j
                    return jnp.sum(jnp.where(pick, gates, 0.0), axis=-1, keepdims=True)

                o = (gate(0) * oc_ref[pl.ds(r0 + jj * tq, tq), :] + gate(1) * o_slc[rows_j, :]
                     + gate(2) * o_win[rows_j, :])
                zj = z_ref[:, pl.ds(c0, HEAD_DIM)].astype(F32)
                y_ref[:, pl.ds(c0, HEAD_DIM)] = (o * _silu(zj)).astype(y_ref.dtype)
            return carry

        lax.fori_loop(0, NSA_HPG // hs, head_body, 0)

    for n in range(1, seq // tq + 1):
        pl.when(qi == n - 1)(functools.partial(variant, n))


def _alibi_slopes():
    return 2.0 ** (-ALIBI_MAX_EXP * jnp.arange(1, NSA_HEADS + 1, dtype=F32) / NSA_HEADS)


def _nsa_tables(seq, n_cmp_pad):
    slopes = _alibi_slopes()
    n_sel = seq // SEL_LEN
    parts = []
    rest = slopes * (HEAD_DIM ** 0.5)
    for _ in range(3):
        part = rest.astype(BF16).astype(F32)
        parts.append(part)
        rest = rest - part
    slx = jnp.zeros((NSA_HEADS, LANES), F32)
    for c, part in enumerate(parts + parts):
        slx = slx.at[:, n_sel + c].set(part)
    slx = slx.reshape(NSA_GROUPS, NSA_HPG, LANES)
    lane = jnp.arange(LANES, dtype=jnp.int32)[None, :]

    def position_lanes(pos, base):
        hi = ((pos // SEL_LEN) * SEL_LEN).astype(F32)[:, None]
        lo = (pos % SEL_LEN).astype(F32)[:, None]
        out = jnp.where((lane >= n_sel) & (lane < n_sel + 3), hi, base)
        return jnp.where((lane >= n_sel + 3) & (lane < n_sel + 6), lo, out).astype(BF16)

    key = jnp.arange(seq, dtype=jnp.int32)
    kx = position_lanes(key, jnp.where(lane == (key // SEL_LEN)[:, None], 1.0, 0.0))
    cmp_end = jnp.arange(n_cmp_pad, dtype=jnp.int32) * CMP_STRIDE + (CMP_LEN - 1)
    cx = position_lanes(cmp_end, jnp.zeros((n_cmp_pad, LANES), F32))
    return slx, kx, cx


def _nsa_attention(qkv, kv_cmp, gl, z, tq=NSA_QUERY_TILE):
    bsz, s, _ = qkv.shape
    gw = NSA_HPG * HEAD_DIM
    ncp = kv_cmp.shape[3]
    kvb = NSA_HEADS
    rows = NSA_HPG * tq
    assert WINDOW % tq == 0 and s // SEL_LEN + 6 <= LANES and tq % SEL_LEN == 0
    slx, kx, cx = _nsa_tables(s, ncp)

    def kv_spec(which):
        return pl.BlockSpec((None, s, HEAD_DIM), lambda b, g, i: (b, 0, kvb + which * NSA_GROUPS + g))

    return pl.pallas_call(
        functools.partial(_nsa_attn_kernel, tq=tq, seq=s),
        out_shape=jax.ShapeDtypeStruct((bsz, s, NSA_HEADS * HEAD_DIM), BF16),
        grid=(bsz, NSA_GROUPS, s // tq),
        in_specs=[
            pl.BlockSpec((None, NSA_HPG, LANES), lambda b, g, i: (g, 0, 0)),
            pl.BlockSpec((s, LANES), lambda b, g, i: (0, 0)),
            pl.BlockSpec((ncp, LANES), lambda b, g, i: (0, 0)),
            pl.BlockSpec((None, tq, gw), lambda b, g, i: (b, i, g)),
            pl.BlockSpec((None, None, None, ncp, HEAD_DIM), lambda b, g, i: (b, 0, g, 0, 0)),
            pl.BlockSpec((None, None, None, ncp, HEAD_DIM), lambda b, g, i: (b, 1, g, 0, 0)),
            kv_spec(2), kv_spec(3), kv_spec(4), kv_spec(5),
            pl.BlockSpec((None, tq, LANES), lambda b, g, i: (b, i, 0)),
            pl.BlockSpec((None, tq, gw), lambda b, g, i: (b, i, g)),
        ],
        out_specs=pl.BlockSpec((None, tq, gw), lambda b, g, i: (b, i, g)),
        scratch_shapes=[
            pltpu.VMEM((s, 2 * HEAD_DIM), BF16),
            pltpu.VMEM((s, 2 * HEAD_DIM), BF16),
            pltpu.VMEM((ncp, 2 * HEAD_DIM), BF16),
            pltpu.VMEM((rows, 2 * HEAD_DIM), BF16),
            pltpu.VMEM((rows, HEAD_DIM), F32),
        ],
        compiler_params=_cparams(("parallel", "parallel", "arbitrary")),
        name="nsa_attention",
    )(slx, kx, cx, qkv, kv_cmp, kv_cmp, qkv, qkv, qkv, qkv, gl, z)


def _nsa_mixer(h2d, bsz, seq, layer, w_in, cmp_pe, cmp_w1, cmp_w2, w_out):
    inner = NSA_HEADS * HEAD_DIM
    kv = NSA_GROUPS * HEAD_DIM
    n_qkv = inner + 6 * kv
    n_gl = 3 * NSA_HEADS
    assert n_gl < LANES and n_qkv % LANES == 0
    w_in_t = jnp.swapaxes(w_in, 1, 2)
    qkv = _proj(h2d, w_in_t, BF16, 0, n_qkv, layer, transposed=True).reshape(bsz, seq, n_qkv)
    gl = _proj(h2d, w_in_t, F32, n_qkv, LANES, layer, transposed=True).reshape(bsz, seq, LANES)
    z = _proj(h2d, w_in_t, BF16, n_qkv, inner, layer, shift=n_gl, transposed=True).reshape(bsz, seq, inner)
    kv_cmp = _nsa_compress(qkv, cmp_pe[layer], cmp_w1[layer], cmp_w2[layer])
    y = _nsa_attention(qkv, kv_cmp, gl, z)
    return _proj(y.reshape(bsz * seq, inner), w_out, BF16, layer=layer)


def _rg_kernel(xb_ref, z_ref, cw_ref, cb_ref, gw_ref, gb_ref, lam_ref, y_ref, xpad_ref, h_ref, *, ts):
    si = pl.program_id(2)
    halo = SUBLANES

    @pl.when(si == 0)
    def _():
        xpad_ref[0:halo, :] = jnp.zeros((halo, xpad_ref.shape[1]), F32)
        h_ref[...] = jnp.zeros_like(h_ref)

    x = xb_ref[...]
    xpad_ref[halo:halo + ts, :] = x
    xc = cb_ref[...] + cw_ref[RG_CONV - 1:RG_CONV, :] * x
    for k in range(RG_CONV - 1):
        shift = RG_CONV - 1 - k
        xc = xc + cw_ref[k:k + 1, :] * xpad_ref[halo - shift:halo - shift + ts, :]
    xpad_ref[0:halo, :] = x[ts - halo:ts, :]

    xcb = xc.astype(BF16)
    gate_i = _sigmoid(jnp.dot(xcb, gw_ref[0], preferred_element_type=F32) + gb_ref[0])
    gate_r = _sigmoid(jnp.dot(xcb, gw_ref[1], preferred_element_type=F32) + gb_ref[1])
    nl = -lam_ref[...]
    softplus = jnp.maximum(nl, 0.0) + jnp.log(1.0 + jnp.exp(-jnp.abs(nl)))
    log_a = (-RG_C) * gate_r * softplus
    a = jnp.exp(log_a)
    one_m_a2 = 1.0 - a * a
    mult = jnp.where(one_m_a2 > 0.0, one_m_a2 * lax.rsqrt(one_m_a2), 0.0)
    row = lax.broadcasted_iota(jnp.int32, (ts, 1), 0)
    mult = jnp.where((row + si * ts) == 0, 1.0, mult)
    u = mult * gate_i * xc

    row_in_group = row & (SUBLANES - 1)
    d = 1
    while d < SUBLANES:
        keep = row_in_group >= d
        a_sh = pltpu.roll(a, d, 0)
        u_sh = pltpu.roll(u, d, 0)
        u = jnp.where(keep, a * u_sh + u, u)
        a = jnp.where(keep, a * a_sh, a)
        d *= 2
    carry = h_ref[...]
    groups = []
    for g in range(ts // SUBLANES):
        rows_g = slice(g * SUBLANES, (g + 1) * SUBLANES)
        h_g = a[rows_g, :] * carry + u[rows_g, :]
        groups.append(h_g)
        carry = h_g[SUBLANES - 1:SUBLANES, :]
    h_ref[...] = carry
    hs = jnp.concatenate(groups, axis=0)
    y_ref[...] = (hs * _silu(z_ref[...].astype(F32))).astype(y_ref.dtype)


def _rg_core(xb, z, conv_w, conv_b, gate_w, gate_b, lam, ts=RG_TILE):
    bsz, s, w = xb.shape
    cb = w // RG_BLOCKS
    return pl.pallas_call(
        functools.partial(_rg_kernel, ts=ts),
        out_shape=jax.ShapeDtypeStruct((bsz, s, w), BF16),
        grid=(bsz, RG_BLOCKS, s // ts),
        in_specs=[
            pl.BlockSpec((None, ts, cb), lambda b, n, i: (b, i, n)),
            pl.BlockSpec((None, ts, cb), lambda b, n, i: (b, i, n)),
            pl.BlockSpec((RG_CONV, cb), lambda b, n, i: (0, n)),
            pl.BlockSpec((1, cb), lambda b, n, i: (0, n)),
            pl.BlockSpec((2, None, cb, cb), lambda b, n, i: (0, n, 0, 0)),
            pl.BlockSpec((2, None, 1, cb), lambda b, n, i: (0, n, 0, 0)),
            pl.BlockSpec((1, cb), lambda b, n, i: (0, n)),
        ],
        out_specs=pl.BlockSpec((None, ts, cb), lambda b, n, i: (b, i, n)),
        scratch_shapes=[pltpu.VMEM((ts + SUBLANES, cb), F32), pltpu.VMEM((1, cb), F32)],
        compiler_params=_cparams(("parallel", "parallel", "arbitrary")),
        name="rglru_core",
    )(xb, z, conv_w.astype(F32), conv_b.reshape(1, w).astype(F32), gate_w.astype(BF16),
      gate_b.reshape(2, RG_BLOCKS, 1, cb).astype(F32), lam.reshape(1, w).astype(F32))


def _rglru_mixer(h2d, bsz, seq, w_in, conv_w, conv_b, gate_w, gate_b, lam, w_out):
    width = w_out.shape[0]
    xb = _proj(h2d, w_in, F32, 0, width).reshape(bsz, seq, width)
    z = _proj(h2d, w_in, BF16, width, width).reshape(bsz, seq, width)
    y = _rg_core(xb, z, conv_w, conv_b, gate_w, gate_b, lam)
    return _proj(y.reshape(bsz * seq, width), w_out, BF16)


def _hg_kernel(q_ref, f_ref, v_ref, g_ref, lbl_ref, ng_ref, y_ref, state_ref, b_ref, k_ref, *, tc, layer, heads):
    @pl.when(pl.program_id(2) == 0)
    def _():
        state_ref[...] = jnp.zeros_like(state_ref)

    dk = q_ref.shape[1] // heads
    dv = v_ref.shape[1] // heads
    for hh in range(heads):
        ks = slice(hh * dk, (hh + 1) * dk)
        vs = slice(hh * dv, (hh + 1) * dv)
        _hg_head(q_ref.at[:, ks], f_ref.at[:, ks], v_ref.at[:, vs], g_ref.at[:, vs], lbl_ref.at[:, ks], ng_ref,
                 y_ref.at[:, vs], state_ref.at[hh], b_ref.at[:, ks], k_ref.at[:, ks], tc=tc, layer=layer)


def _hg_head(q_ref, f_ref, v_ref, g_ref, lbl_ref, ng_ref, y_ref, state_ref, b_ref, k_ref, *, tc, layer):
    dk = q_ref.shape[1]
    nt = (((1,), (1,)), ((), ()))
    tn = (((0,), (0,)), ((), ()))

    lg = lbl_ref[...]
    e = jnp.exp(lg - jnp.max(lg, axis=0, keepdims=True))
    pl_sm = e * (1.0 / jnp.sum(e, axis=0, keepdims=True))
    lb = jnp.zeros((1, dk), F32)
    for r in range(1, layer + 1):
        lb = lb + pl_sm[r:r + 1, :]

    q = _silu(q_ref[...])
    fg = lb + (1.0 - lb) * _sigmoid(f_ref[...])
    kk = 1.0 - fg
    b = jnp.log2(fg)
    row = lax.broadcasted_iota(jnp.int32, (tc, 1), 0)
    rc = row & (HG_CHUNK - 1)
    d = 1
    while d < HG_CHUNK:
        b = b + jnp.where(rc >= d, pltpu.roll(b, d, 0), 0.0)
        d *= 2
    b_ref[...] = b
    k_ref[...] = kk

    nsub = tc // HG_SUB
    lane = lax.broadcasted_iota(jnp.int32, (dk, LANES), 1)
    acc = jnp.zeros((tc, LANES), F32)
    for s in range(HG_SUB):
        b_s = jnp.concatenate(
            [jnp.broadcast_to(b_ref[i * HG_SUB + s:i * HG_SUB + s + 1, :], (HG_SUB, dk)) for i in range(nsub)], axis=0)
        k_s = jnp.concatenate(
            [jnp.broadcast_to(k_ref[i * HG_SUB + s:i * HG_SUB + s + 1, :], (HG_SUB, dk)) for i in range(nsub)], axis=0)
        m_s = q * jnp.exp2(jnp.minimum(b - b_s, 0.0)) * k_s
        w_s = jnp.where(((lane & (HG_SUB - 1)) == s) & (lane < HG_CHUNK), 1.0, 0.0).astype(BF16)
        acc = acc + jnp.dot(m_s.astype(BF16), w_s, preferred_element_type=F32)

    col = lax.broadcasted_iota(jnp.int32, (HG_CHUNK, HG_CHUNK), 1)
    rw = lax.broadcasted_iota(jnp.int32, (HG_CHUNK, HG_CHUNK), 0)
    col_sub = lax.shift_right_logical(col, HG_SUB_SHIFT)
    rw_sub = lax.shift_right_logical(rw, HG_SUB_SHIFT)
    diag_mask = (col_sub == rw_sub) & (col <= rw)

    state = state_ref[...]
    nsc = HG_CHUNK // HG_SUB
    for c in range(tc // HG_CHUNK):
        r0 = c * HG_CHUNK
        bc = b[r0:r0 + HG_CHUNK, :]
        qc = q[r0:r0 + HG_CHUNK, :]
        kc = kk[r0:r0 + HG_CHUNK, :]
        vc = v_ref[r0:r0 + HG_CHUNK, :]
        o = lax.dot_general((qc * jnp.exp2(bc)).astype(BF16), state.astype(BF16), nt, preferred_element_type=F32)
        blocks = [jnp.zeros((HG_SUB, HG_CHUNK), F32)]
        for i in range(1, nsc):
            n_prev = i * HG_SUB
            r_i = bc[n_prev - 1:n_prev, :]
            q_i = (qc[n_prev:n_prev + HG_SUB, :] * jnp.exp2(bc[n_prev:n_prev + HG_SUB, :] - r_i)).astype(BF16)
            k_i = kc[0:n_prev, :] * jnp.exp2(r_i - bc[0:n_prev, :])
            k_i = jnp.concatenate([k_i, jnp.zeros((HG_CHUNK - n_prev, dk), F32)], axis=0).astype(BF16)
            blocks.append(lax.dot_general(q_i, k_i, nt, preferred_element_type=F32))
        att_off = jnp.concatenate(blocks, axis=0)
        att = jnp.where(diag_mask, acc[r0:r0 + HG_CHUNK, 0:HG_CHUNK],
                        jnp.where(col_sub < rw_sub, att_off, 0.0))
        o = o + jnp.dot(att.astype(BF16), vc, preferred_element_type=F32)
        b_last = bc[HG_CHUNK - 1:HG_CHUNK, :]
        k_dec = (kc * jnp.exp2(b_last - bc)).astype(BF16)
        state = state * jnp.exp2(b_last) + lax.dot_general(vc, k_dec, tn, preferred_element_type=F32)
        ms = jnp.mean(o * o, axis=-1, keepdims=True)
        on = o * lax.rsqrt(ms + NORM_EPS) * ng_ref[...]
        gc = g_ref[r0:r0 + HG_CHUNK, :].astype(F32)
        y_ref[r0:r0 + HG_CHUNK, :] = (on * _silu(gc)).astype(y_ref.dtype)
    state_ref[...] = state


def _hg_core(qf, vg, lb_logits, norm_gain, layer, tc=HG_TILE, heads=HG_HEADS_PER_STEP):
    bsz, s, _ = qf.shape
    dk = qf.shape[2] // (2 * HG_HEADS)
    dv = vg.shape[2] // (2 * HG_HEADS)
    nl = lb_logits.shape[0]
    groups = HG_HEADS // heads
    wk, wv = heads * dk, heads * dv
    return pl.pallas_call(
        functools.partial(_hg_kernel, tc=tc, layer=layer, heads=heads),
        out_shape=jax.ShapeDtypeStruct((bsz, s, HG_HEADS * dv), BF16),
        grid=(bsz, groups, s // tc),
        in_specs=[
            pl.BlockSpec((None, tc, wk), lambda b, h, i: (b, i, h)),
            pl.BlockSpec((None, tc, wk), lambda b, h, i: (b, i, groups + h)),
            pl.BlockSpec((None, tc, wv), lambda b, h, i: (b, i, h)),
            pl.BlockSpec((None, tc, wv), lambda b, h, i: (b, i, groups + h)),
            pl.BlockSpec((nl, wk), lambda b, h, i: (0, h)),
            pl.BlockSpec((1, dv), lambda b, h, i: (0, 0)),
        ],
        out_specs=pl.BlockSpec((None, tc, wv), lambda b, h, i: (b, i, h)),
        scratch_shapes=[pltpu.VMEM((heads, dv, dk), F32), pltpu.VMEM((tc, wk), F32), pltpu.VMEM((tc, wk), F32)],
        compiler_params=_cparams(("parallel", "parallel", "arbitrary")),
        name="hgrn2_core",
    )(qf, qf, vg, vg, lb_logits.astype(F32), norm_gain.reshape(1, dv).astype(F32))


def _hgrn2_mixer(h2d, bsz, seq, w_in, lb_logits, layer, norm_gain, w_out):
    val = w_out.shape[0]
    key = (w_in.shape[1] - 2 * val) // 2
    qf = _proj(h2d, w_in, F32, 0, 2 * key).reshape(bsz, seq, 2 * key)
    vg = _proj(h2d, w_in, BF16, 2 * key, 2 * val).reshape(bsz, seq, 2 * val)
    y = _hg_core(qf, vg, lb_logits, norm_gain, layer)
    return _proj(y.reshape(bsz * seq, val), w_out, BF16)


def kernel(x, pre_norm_gain, post_norm_gain, nsa_w_in, nsa_cmp_pe, nsa_cmp_w1, nsa_cmp_w2, nsa_w_out,
           rg_w_in, rg_conv_w, rg_conv_b, rg_gate_w, rg_gate_b, rg_lambda, rg_w_out,
           hg_w_in, hg_lb_logits, hg_norm_gain, hg_w_out):
    bsz, seq, d = x.shape
    depth = pre_norm_gain.shape[0]
    x2d = x.reshape(bsz * seq, d)
    h = _prenorm(x2d, pre_norm_gain[0])
    for i in range(depth):
        kind, j = i % 3, i // 3
        if kind == 0:
            y = _nsa_mixer(h, bsz, seq, j, nsa_w_in, nsa_cmp_pe, nsa_cmp_w1, nsa_cmp_w2, nsa_w_out)
        elif kind == 1:
            y = _rglru_mixer(h, bsz, seq, rg_w_in[j], rg_conv_w[j], rg_conv_b[j], rg_gate_w[j], rg_gate_b[j],
                             rg_lambda[j], rg_w_out[j])
        else:
            y = _hgrn2_mixer(h, bsz, seq, hg_w_in[j], hg_lb_logits, i, hg_norm_gain[j], hg_w_out[j])
        next_gain = pre_norm_gain[i + 1] if i + 1 < depth else None
        x2d, h = _postnorm_residual(x2d, y, post_norm_gain[i], next_gain)
    return x2d.reshape(bsz, seq, d)
```

```python
import functools
import math

import jax
import jax.numpy as jnp
from jax import lax
from jax.experimental import pallas as pl
from jax.experimental.pallas import tpu as pltpu

F32 = jnp.float32
BF16 = jnp.bfloat16

NORM_EPS = 1e-6
NEG_INF = -1e30
FORCE_SCORE = 1e6

LANES = 128
SUBLANES = 8
V7X_VMEM_LIMIT_BYTES = 56 * 1024 * 1024

NSA_HEADS = 32
NSA_GROUPS = 4
NSA_HPG = NSA_HEADS // NSA_GROUPS
HEAD_DIM = 128
CMP_LEN = 32
CMP_STRIDE = 16
SEL_LEN = 64
SEL_SHIFT = 6
SEL_TOPK = 16
WINDOW = 512
ALIBI_MAX_EXP = 8.0
NSA_MASK_BIG = 2.0 ** 40
NSA_STACK = 4

RG_BLOCKS = 16
RG_CONV = 4
RG_C = 8.0

HG_HEADS = 32
HG_CHUNK = 64
HG_SUB = 8
HG_SUB_SHIFT = 3


NORM_ROWS = 256
PROJ_ROWS = 1024
PROJ_COLS = 512
NSA_QUERY_TILE = 256
RG_TILE = 1024
HG_TILE = 512
HG_HEADS_PER_STEP = 2


def _cparams(sem):
    return pltpu.CompilerParams(dimension_semantics=sem, vmem_limit_bytes=V7X_VMEM_LIMIT_BYTES)


def _sigmoid(x):
    return 0.5 * jnp.tanh(0.5 * x) + 0.5


def _silu(x):
    return x * _sigmoid(x)


def _prenorm_kernel(x_ref, g_ref, o_ref):
    x = x_ref[...]
    ms = jnp.mean(x * x, axis=-1, keepdims=True)
    o_ref[...] = (x * lax.rsqrt(ms + NORM_EPS) * g_ref[...]).astype(o_ref.dtype)


def _prenorm(x2d, gain, tm=NORM_ROWS):
    t, d = x2d.shape
    return pl.pallas_call(
        _prenorm_kernel,
        out_shape=jax.ShapeDtypeStruct((t, d), BF16),
        grid=(t // tm,),
        in_specs=[pl.BlockSpec((tm, d), lambda i: (i, 0)), pl.BlockSpec((1, d), lambda i: (0, 0))],
        out_specs=pl.BlockSpec((tm, d), lambda i: (i, 0)),
        compiler_params=_cparams(("parallel",)),
        name="prenorm",
    )(x2d, gain.reshape(1, d).astype(F32))


def _postnorm_kernel(x_ref, y_ref, g_ref, *rest):
    y = y_ref[...].astype(F32)
    ms = jnp.mean(y * y, axis=-1, keepdims=True)
    xn = x_ref[...] + y * lax.rsqrt(ms + NORM_EPS) * g_ref[...]
    if len(rest) == 1:
        rest[0][...] = xn
    else:
        gn_ref, o_ref, h_ref = rest
        o_ref[...] = xn
        ms2 = jnp.mean(xn * xn, axis=-1, keepdims=True)
        h_ref[...] = (xn * lax.rsqrt(ms2 + NORM_EPS) * gn_ref[...]).astype(h_ref.dtype)


def _postnorm_residual(x2d, y2d, gain, next_gain=None, tm=NORM_ROWS):
    t, d = x2d.shape
    row = pl.BlockSpec((tm, d), lambda i: (i, 0))
    vec = pl.BlockSpec((1, d), lambda i: (0, 0))
    operands = [x2d, y2d, gain.reshape(1, d).astype(F32)]
    in_specs = [row, row, vec]
    out_shape = jax.ShapeDtypeStruct((t, d), F32)
    out_specs = row
    if next_gain is not None:
        operands.append(next_gain.reshape(1, d).astype(F32))
        in_specs.append(vec)
        out_shape = (out_shape, jax.ShapeDtypeStruct((t, d), BF16))
        out_specs = (row, row)
    res = pl.pallas_call(
        _postnorm_kernel,
        out_shape=out_shape,
        grid=(t // tm,),
        in_specs=in_specs,
        out_specs=out_specs,
        compiler_params=_cparams(("parallel",)),
        name="postnorm_residual",
    )(*operands)
    return res if next_gain is not None else (res, None)


def _proj_kernel(a_ref, w_ref, *rest, shift, transposed):
    if shift:
        wn_ref, o_ref, wb_ref = rest
    else:
        o_ref, wb_ref = rest

    @pl.when(pl.program_id(1) == 0)
    def _():
        if shift:
            tn = wb_ref.shape[0]
            wb_ref[0:tn - shift, :] = w_ref[shift:tn, :].astype(BF16)
            wb_ref[tn - shift:tn, :] = wn_ref[0:shift, :].astype(BF16)
        else:
            wb_ref[...] = w_ref[...].astype(BF16)

    if transposed:
        acc = lax.dot_general(a_ref[...], wb_ref[...], (((1,), (1,)), ((), ())), preferred_element_type=F32)
    else:
        acc = jnp.dot(a_ref[...], wb_ref[...], preferred_element_type=F32)
    o_ref[...] = acc.astype(o_ref.dtype)


def _proj(a, w, out_dtype, col0=0, ncols=None, layer=0, shift=0, transposed=False, tm=PROJ_ROWS, tn=PROJ_COLS):
    if w.ndim == 2:
        w = w.reshape((1,) + w.shape)
    m, kdim = a.shape
    k_axis, n_axis = (2, 1) if transposed else (1, 2)
    ncols = w.shape[n_axis] - col0 if ncols is None else ncols
    tm, tn = min(tm, m), min(tn, ncols)
    assert m % tm == 0 and ncols % tn == 0 and col0 % tn == 0 and w.shape[k_axis] == kdim
    assert shift == 0 or (transposed and 0 < shift < LANES and shift % (2 * SUBLANES) == 0 and tn % LANES == 0)
    cb0 = col0 // tn
    if transposed:
        w_block = (None, tn, kdim)
        w_spec = pl.BlockSpec(w_block, lambda j, i: (layer, cb0 + j, 0))
    else:
        w_block = (None, kdim, tn)
        w_spec = pl.BlockSpec(w_block, lambda j, i: (layer, 0, cb0 + j))
    in_specs = [pl.BlockSpec((tm, kdim), lambda j, i: (i, 0)), w_spec]
    operands = [a, w]
    if shift:
        lane_tiles = tn // LANES
        in_specs.append(pl.BlockSpec((None, LANES, kdim), lambda j, i: (layer, (cb0 + j + 1) * lane_tiles, 0)))
        operands.append(w)
    return pl.pallas_call(
        functools.partial(_proj_kernel, shift=shift, transposed=transposed),
        out_shape=jax.ShapeDtypeStruct((m, ncols), out_dtype),
        grid=(ncols // tn, m // tm),
        in_specs=in_specs,
        out_specs=pl.BlockSpec((tm, tn), lambda j, i: (i, j)),
        scratch_shapes=[pltpu.VMEM(w_block[1:], BF16)],
        compiler_params=_cparams(("parallel", "arbitrary")),
        name="projection",
    )(*operands)


def _gelu_tanh(x):
    return 0.5 * x * (1.0 + jnp.tanh(math.sqrt(2.0 / math.pi) * (x + 0.044715 * (x * x * x))))


def _compress_kernel(x_ref, pe_ref, w1_ref, w2_ref, o_ref, xf_ref):
    s = x_ref.shape[0]
    nch = s // CMP_STRIDE
    xf_ref[...] = x_ref[...].astype(F32)
    acc_lo = jnp.zeros((nch, HEAD_DIM), F32)
    acc_hi = jnp.zeros((nch, HEAD_DIM), F32)
    for l in range(CMP_STRIDE):
        xl = xf_ref[pl.ds(l, nch, stride=CMP_STRIDE), :]
        lo_in = (xl + pe_ref[l:l + 1, :]).astype(BF16)
        hi_in = (xl + pe_ref[CMP_STRIDE + l:CMP_STRIDE + l + 1, :]).astype(BF16)
        acc_lo += jnp.dot(lo_in, w1_ref[l * HEAD_DIM:(l + 1) * HEAD_DIM, :], preferred_element_type=F32)
        acc_hi += jnp.dot(hi_in, w1_ref[(CMP_STRIDE + l) * HEAD_DIM:(CMP_STRIDE + l + 1) * HEAD_DIM, :],
                          preferred_element_type=F32)
    pre = acc_lo + pltpu.roll(acc_hi, nch - 1, 0)
    hid = _gelu_tanh(pre).astype(BF16)
    o_ref[...] = jnp.dot(hid, w2_ref[...], preferred_element_type=F32).astype(o_ref.dtype)


def _nsa_compress(qkv, pe, w1, w2):
    bsz, s, _ = qkv.shape
    nch = s // CMP_STRIDE
    kv_block0 = NSA_HEADS
    return pl.pallas_call(
        _compress_kernel,
        out_shape=jax.ShapeDtypeStruct((bsz, 2, NSA_GROUPS, nch, HEAD_DIM), BF16),
        grid=(bsz, 2, NSA_GROUPS),
        in_specs=[
            pl.BlockSpec((None, s, HEAD_DIM), lambda b, w, g: (b, 0, kv_block0 + w * NSA_GROUPS + g)),
            pl.BlockSpec((None, CMP_LEN, HEAD_DIM), lambda b, w, g: (w, 0, 0)),
            pl.BlockSpec((None, CMP_LEN * HEAD_DIM, HEAD_DIM), lambda b, w, g: (w, 0, 0)),
            pl.BlockSpec((None, HEAD_DIM, HEAD_DIM), lambda b, w, g: (w, 0, 0)),
        ],
        out_specs=pl.BlockSpec((None, None, None, nch, HEAD_DIM), lambda b, w, g: (b, w, g, 0, 0)),
        scratch_shapes=[pltpu.VMEM((s, HEAD_DIM), F32)],
        compiler_params=_cparams(("parallel", "parallel", "parallel")),
        name="nsa_compress",
    )(qkv, pe.astype(F32), w1.astype(BF16), w2.astype(BF16))


def _split3(x):
    hi = x.astype(BF16)
    r1 = x - hi.astype(F32)
    mid = r1.astype(BF16)
    lo = (r1 - mid.astype(F32)).astype(BF16)
    return hi, mid, lo


def _nsa_attn_kernel(slx_ref, kx_ref, cx_ref, q_ref, kcmp_ref, vcmp_ref, ks_ref, vs_ref, kw_ref, vw_ref,
                     gl_ref, z_ref, y_ref, ksx_ref, kwx_ref, kcx_ref, qx_ref, oc_ref, *, tq, seq):
    qi = pl.program_id(2)
    t0 = qi * tq
    scale = HEAD_DIM ** -0.5
    n_cmp_pad = kcmp_ref.shape[0]
    n_sel = seq // SEL_LEN
    rows = NSA_HPG * tq
    nt = (((1,), (1,)), ((), ()))

    @pl.when(qi == 0)
    def _():
        kx = kx_ref[...]
        lane_k = lax.broadcasted_iota(jnp.int32, kx.shape, 1)
        ksx_ref[:, 0:HEAD_DIM] = ks_ref[...]
        ksx_ref[:, HEAD_DIM:] = kx
        kwx_ref[:, 0:HEAD_DIM] = kw_ref[...]
        kwx_ref[:, HEAD_DIM:] = jnp.where(lane_k >= n_sel, kx, jnp.zeros_like(kx))

        kcx_ref[:, 0:HEAD_DIM] = kcmp_ref[...]
        kcx_ref[:, HEAD_DIM:] = cx_ref[...]

    tpos_i = t0 + lax.broadcasted_iota(jnp.int32, (tq, 1), 0)
    exp2_scale = scale * math.log2(math.e)

    for j in range(NSA_HPG):
        qx_ref[j * tq:(j + 1) * tq, 0:HEAD_DIM] = q_ref[:, j * HEAD_DIM:(j + 1) * HEAD_DIM]
        qx_ref[j * tq:(j + 1) * tq, HEAD_DIM:] = jnp.broadcast_to(slx_ref[j:j + 1, :], (tq, LANES)).astype(BF16)

    t_rows = t0 + (lax.broadcasted_iota(jnp.int32, (rows, 1), 0) & (tq - 1))
    cmp_end = lax.broadcasted_iota(jnp.int32, (1, n_cmp_pad), 1) * CMP_STRIDE + (CMP_LEN - 1)
    valid_c = t_rows >= cmp_end
    s = lax.dot_general(qx_ref[...], kcx_ref[...], nt, preferred_element_type=F32)
    s = jnp.where(valid_c, s, NEG_INF)
    m = jnp.max(s, axis=-1, keepdims=True)
    p = jnp.exp2((s - m) * exp2_scale)
    l = jnp.sum(p, axis=-1, keepdims=True)
    p = jnp.where(valid_c, p * (1.0 / l), 0.0)
    oc_ref[...] = jnp.dot(p.astype(BF16), vcmp_ref[...], preferred_element_type=F32)
    pg = p[0:tq, :]
    for j in range(1, NSA_HPG):
        pg = pg + p[j * tq:(j + 1) * tq, :]

    n_idx = lax.broadcasted_iota(jnp.int32, (n_cmp_pad, LANES), 0)
    j_idx = lax.broadcasted_iota(jnp.int32, (n_cmp_pad, LANES), 1)
    dd = n_idx - (SEL_LEN // CMP_STRIDE) * j_idx + (CMP_LEN // CMP_STRIDE - 1)
    pool = jnp.where((dd == 0) | (dd == 4), 1.0, jnp.where((dd >= 1) & (dd <= 3), 2.0, 0.0)).astype(BF16)
    p_slc = jnp.zeros((tq, LANES), F32)
    for part in _split3(pg):
        p_slc = p_slc + jnp.dot(part, pool, preferred_element_type=F32)

    blk = lax.broadcasted_iota(jnp.int32, (1, LANES), 1)
    cur = lax.shift_right_logical(tpos_i, SEL_SHIFT)
    forced = (blk == 0) | (blk == cur) | (blk == cur - 1)
    future = blk > cur
    score = jnp.where(forced, FORCE_SCORE, jnp.where(future, -1.0, p_slc))
    score = jnp.where(blk < n_sel, score, -2.0)
    score_t = score.T[0:n_sel, :]
    blk_t = lax.broadcasted_iota(jnp.int32, (n_sel, 1), 0)
    rank = jnp.zeros((n_sel, tq), F32)
    for jp in range(n_sel):
        other = score_t[jp:jp + 1, :]
        ahead = (other > score_t) | ((other == score_t) & (blk_t > jp))
        rank = rank + jnp.where(ahead, 1.0, 0.0)
    pen_t = jnp.where(rank < float(min(SEL_TOPK, n_sel)), 0.0, -NSA_MASK_BIG)
    penalty = jnp.concatenate([pen_t, jnp.zeros((LANES - n_sel, tq), F32)], axis=0).T

    for j in range(NSA_HPG):
        qx_ref[j * tq:(j + 1) * tq, HEAD_DIM:] = (penalty + slx_ref[j:j + 1, :]).astype(BF16)

    hs = NSA_STACK
    t_loc = lax.broadcasted_iota(jnp.int32, (hs * tq, 1), 0) & (tq - 1)
    k_loc = lax.broadcasted_iota(jnp.int32, (1, tq), 1)
    causal = k_loc <= t_loc
    far_ok = k_loc > t_loc
    n_back = WINDOW // tq
    gates = _sigmoid(gl_ref[...].astype(F32))
    lane_g = lax.broadcasted_iota(jnp.int32, (1, LANES), 1)
    head0 = pl.program_id(1) * NSA_HPG

    def attend(qx, pieces):
        ss = []
        for kx_p, _, mask in pieces:
            s = lax.dot_general(qx, kx_p, nt, preferred_element_type=F32)
            ss.append(s if mask is None else jnp.where(mask, s, NEG_INF))
        m = functools.reduce(jnp.maximum, [jnp.max(s, axis=-1, keepdims=True) for s in ss])
        l = 0.0
        o = 0.0
        for s, (_, v_p, _) in zip(ss, pieces):
            p = jnp.exp2((s - m) * exp2_scale)
            l = l + jnp.sum(p, axis=-1, keepdims=True)
            o = o + jnp.dot(p.astype(BF16), v_p, preferred_element_type=F32)
        return o * (1.0 / l)

    def variant(n):
        d0 = (n - 1) * tq

        def head_body(jg, carry):
            r0 = pl.multiple_of(jg * (hs * tq), hs * tq)
            qx = qx_ref[pl.ds(r0, hs * tq), :]
            sel_pieces = [(ksx_ref[d0:d0 + tq, :], vs_ref[d0:d0 + tq, :], causal)]
            if n > 1:
                sel_pieces.append((ksx_ref[0:d0, :], vs_ref[0:d0, :], None))
            o_slc = attend(qx, sel_pieces)
            win_pieces = [(kwx_ref[d0:d0 + tq, :], vw_ref[d0:d0 + tq, :], causal)]
            for w in range(1, min(n - 1, n_back) + 1):
                k0 = d0 - w * tq
                win_pieces.append((kwx_ref[k0:k0 + tq, :], vw_ref[k0:k0 + tq, :], far_ok if w == n_back else None))
            o_win = attend(qx, win_pieces)

            for jj in range(hs):
                j = jg * hs + jj
                rows_j = slice(jj * tq, (jj + 1) * tq)
                c0 = pl.multiple_of(j * HEAD_DIM, HEAD_DIM)

                def gate(branch):
                    pick = lane_g == branch * NSA_HEADS + head0 + j
                    return jnp.sum(jnp.where(pick, gates, 0.0), axis=-1, keepdims=True)

                o = (gate(0) * oc_ref[pl.ds(r0 + jj * tq, tq), :] + gate(1) * o_slc[rows_j, :]
                     + gate(2) * o_win[rows_j, :])
                zj = z_ref[:, pl.ds(c0, HEAD_DIM)].astype(F32)
                y_ref[:, pl.ds(c0, HEAD_DIM)] = (o * _silu(zj)).astype(y_ref.dtype)
            return carry

        lax.fori_loop(0, NSA_HPG // hs, head_body, 0, unroll=True)

    for n in range(1, seq // tq + 1):
        pl.when(qi == n - 1)(functools.partial(variant, n))


def _alibi_slopes():
    return 2.0 ** (-ALIBI_MAX_EXP * jnp.arange(1, NSA_HEADS + 1, dtype=F32) / NSA_HEADS)


def _nsa_tables(seq, n_cmp_pad):
    slopes = _alibi_slopes()
    n_sel = seq // SEL_LEN
    parts = []
    rest = slopes * (HEAD_DIM ** 0.5)
    for _ in range(3):
        part = rest.astype(BF16).astype(F32)
        parts.append(part)
        rest = rest - part
    slx = jnp.zeros((NSA_HEADS, LANES), F32)
    for c, part in enumerate(parts + parts):
        slx = slx.at[:, n_sel + c].set(part)
    slx = slx.reshape(NSA_GROUPS, NSA_HPG, LANES)
    lane = jnp.arange(LANES, dtype=jnp.int32)[None, :]

    def position_lanes(pos, base):
        hi = ((pos // SEL_LEN) * SEL_LEN).astype(F32)[:, None]
        lo = (pos % SEL_LEN).astype(F32)[:, None]
        out = jnp.where((lane >= n_sel) & (lane < n_sel + 3), hi, base)
        return jnp.where((lane >= n_sel + 3) & (lane < n_sel + 6), lo, out).astype(BF16)

    key = jnp.arange(seq, dtype=jnp.int32)
    kx = position_lanes(key, jnp.where(lane == (key // SEL_LEN)[:, None], 1.0, 0.0))
    cmp_end = jnp.arange(n_cmp_pad, dtype=jnp.int32) * CMP_STRIDE + (CMP_LEN - 1)
    cx = position_lanes(cmp_end, jnp.zeros((n_cmp_pad, LANES), F32))
    return slx, kx, cx


def _nsa_attention(qkv, kv_cmp, gl, z, tq=NSA_QUERY_TILE):
    bsz, s, _ = qkv.shape
    gw = NSA_HPG * HEAD_DIM
    ncp = kv_cmp.shape[3]
    kvb = NSA_HEADS
    rows = NSA_HPG * tq
    assert WINDOW % tq == 0 and s // SEL_LEN + 6 <= LANES and tq % SEL_LEN == 0
    slx, kx, cx = _nsa_tables(s, ncp)

    def kv_spec(which):
        return pl.BlockSpec((None, s, HEAD_DIM), lambda b, g, i: (b, 0, kvb + which * NSA_GROUPS + g))

    return pl.pallas_call(
        functools.partial(_nsa_attn_kernel, tq=tq, seq=s),
        out_shape=jax.ShapeDtypeStruct((bsz, s, NSA_HEADS * HEAD_DIM), BF16),
        grid=(bsz, NSA_GROUPS, s // tq),
        in_specs=[
            pl.BlockSpec((None, NSA_HPG, LANES), lambda b, g, i: (g, 0, 0)),
            pl.BlockSpec((s, LANES), lambda b, g, i: (0, 0)),
            pl.BlockSpec((ncp, LANES), lambda b, g, i: (0, 0)),
            pl.BlockSpec((None, tq, gw), lambda b, g, i: (b, i, g)),
            pl.BlockSpec((None, None, None, ncp, HEAD_DIM), lambda b, g, i: (b, 0, g, 0, 0)),
            pl.BlockSpec((None, None, None, ncp, HEAD_DIM), lambda b, g, i: (b, 1, g, 0, 0)),
            kv_spec(2), kv_spec(3), kv_spec(4), kv_spec(5),
            pl.BlockSpec((None, tq, LANES), lambda b, g, i: (b, i, 0)),
            pl.BlockSpec((None, tq, gw), lambda b, g, i: (b, i, g)),
        ],
        out_specs=pl.BlockSpec((None, tq, gw), lambda b, g, i: (b, i, g)),
        scratch_shapes=[
            pltpu.VMEM((s, 2 * HEAD_DIM), BF16),
            pltpu.VMEM((s, 2 * HEAD_DIM), BF16),
            pltpu.VMEM((ncp, 2 * HEAD_DIM), BF16),
            pltpu.VMEM((rows, 2 * HEAD_DIM), BF16),
            pltpu.VMEM((rows, HEAD_DIM), F32),
        ],
        compiler_params=_cparams(("parallel", "parallel", "arbitrary")),
        name="nsa_attention",
    )(slx, kx, cx, qkv, kv_cmp, kv_cmp, qkv, qkv, qkv, qkv, gl, z)


def _nsa_mixer(h2d, bsz, seq, layer, w_in, cmp_pe, cmp_w1, cmp_w2, w_out):
    inner = NSA_HEADS * HEAD_DIM
    kv = NSA_GROUPS * HEAD_DIM
    n_qkv = inner + 6 * kv
    n_gl = 3 * NSA_HEADS
    assert n_gl < LANES and n_qkv % LANES == 0
    w_in_t = jnp.swapaxes(w_in, 1, 2)
    qkv = _proj(h2d, w_in_t, BF16, 0, n_qkv, layer, transposed=True).reshape(bsz, seq, n_qkv)
    gl = _proj(h2d, w_in_t, F32, n_qkv, LANES, layer, transposed=True).reshape(bsz, seq, LANES)
    z = _proj(h2d, w_in_t, BF16, n_qkv, inner, layer, shift=n_gl, transposed=True).reshape(bsz, seq, inner)
    kv_cmp = _nsa_compress(qkv, cmp_pe[layer], cmp_w1[layer], cmp_w2[layer])
    y = _nsa_attention(qkv, kv_cmp, gl, z)
    return _proj(y.reshape(bsz * seq, inner), w_out, BF16, layer=layer)


def _rg_kernel(xb_ref, z_ref, cw_ref, cb_ref, gw_ref, gb_ref, lam_ref, y_ref, xpad_ref, h_ref, *, ts):
    si = pl.program_id(2)
    halo = SUBLANES

    @pl.when(si == 0)
    def _():
        xpad_ref[0:halo, :] = jnp.zeros((halo, xpad_ref.shape[1]), F32)
        h_ref[...] = jnp.zeros_like(h_ref)

    x = xb_ref[...]
    xpad_ref[halo:halo + ts, :] = x
    xc = cb_ref[...] + cw_ref[RG_CONV - 1:RG_CONV, :] * x
    for k in range(RG_CONV - 1):
        shift = RG_CONV - 1 - k
        xc = xc + cw_ref[k:k + 1, :] * xpad_ref[halo - shift:halo - shift + ts, :]
    xpad_ref[0:halo, :] = x[ts - halo:ts, :]

    xcb = xc.astype(BF16)
    gate_i = _sigmoid(jnp.dot(xcb, gw_ref[0], preferred_element_type=F32) + gb_ref[0])
    gate_r = _sigmoid(jnp.dot(xcb, gw_ref[1], preferred_element_type=F32) + gb_ref[1])
    nl = -lam_ref[...]
    softplus = jnp.maximum(nl, 0.0) + jnp.log(1.0 + jnp.exp(-jnp.abs(nl)))
    log_a = (-RG_C) * gate_r * softplus
    a = jnp.exp(log_a)
    one_m_a2 = 1.0 - a * a
    mult = jnp.where(one_m_a2 > 0.0, one_m_a2 * lax.rsqrt(one_m_a2), 0.0)
    row = lax.broadcasted_iota(jnp.int32, (ts, 1), 0)
    mult = jnp.where((row + si * ts) == 0, 1.0, mult)
    u = mult * gate_i * xc

    row_in_group = row & (SUBLANES - 1)
    d = 1
    while d < SUBLANES:
        keep = row_in_group >= d
        a_sh = pltpu.roll(a, d, 0)
        u_sh = pltpu.roll(u, d, 0)
        u = jnp.where(keep, a * u_sh + u, u)
        a = jnp.where(keep, a * a_sh, a)
        d *= 2
    carry = h_ref[...]
    groups = []
    for g in range(ts // SUBLANES):
        rows_g = slice(g * SUBLANES, (g + 1) * SUBLANES)
        h_g = a[rows_g, :] * carry + u[rows_g, :]
        groups.append(h_g)
        carry = h_g[SUBLANES - 1:SUBLANES, :]
    h_ref[...] = carry
    hs = jnp.concatenate(groups, axis=0)
    y_ref[...] = (hs * _silu(z_ref[...].astype(F32))).astype(y_ref.dtype)


def _rg_core(xb, z, conv_w, conv_b, gate_w, gate_b, lam, ts=RG_TILE):
    bsz, s, w = xb.shape
    cb = w // RG_BLOCKS
    return pl.pallas_call(
        functools.partial(_rg_kernel, ts=ts),
        out_shape=jax.ShapeDtypeStruct((bsz, s, w), BF16),
        grid=(bsz, RG_BLOCKS, s // ts),
        in_specs=[
            pl.BlockSpec((None, ts, cb), lambda b, n, i: (b, i, n)),
            pl.BlockSpec((None, ts, cb), lambda b, n, i: (b, i, n)),
            pl.BlockSpec((RG_CONV, cb), lambda b, n, i: (0, n)),
            pl.BlockSpec((1, cb), lambda b, n, i: (0, n)),
            pl.BlockSpec((2, None, cb, cb), lambda b, n, i: (0, n, 0, 0)),
            pl.BlockSpec((2, None, 1, cb), lambda b, n, i: (0, n, 0, 0)),
            pl.BlockSpec((1, cb), lambda b, n, i: (0, n)),
        ],
        out_specs=pl.BlockSpec((None, ts, cb), lambda b, n, i: (b, i, n)),
        scratch_shapes=[pltpu.VMEM((ts + SUBLANES, cb), F32), pltpu.VMEM((1, cb), F32)],
        compiler_params=_cparams(("parallel", "parallel", "arbitrary")),
        name="rglru_core",
    )(xb, z, conv_w.astype(F32), conv_b.reshape(1, w).astype(F32), gate_w.astype(BF16),
      gate_b.reshape(2, RG_BLOCKS, 1, cb).astype(F32), lam.reshape(1, w).astype(F32))


def _rglru_mixer(h2d, bsz, seq, w_in, conv_w, conv_b, gate_w, gate_b, lam, w_out):
    width = w_out.shape[0]
    xb = _proj(h2d, w_in, F32, 0, width).reshape(bsz, seq, width)
    z = _proj(h2d, w_in, BF16, width, width).reshape(bsz, seq, width)
    y = _rg_core(xb, z, conv_w, conv_b, gate_w, gate_b, lam)
    return _proj(y.reshape(bsz * seq, width), w_out, BF16)


def _hg_kernel(q_ref, f_ref, v_ref, g_ref, lbl_ref, ng_ref, y_ref, state_ref, b_ref, k_ref, *, tc, layer, heads):
    @pl.when(pl.program_id(2) == 0)
    def _():
        state_ref[...] = jnp.zeros_like(state_ref)

    dk = q_ref.shape[1] // heads
    dv = v_ref.shape[1] // heads
    for hh in range(heads):
        ks = slice(hh * dk, (hh + 1) * dk)
        vs = slice(hh * dv, (hh + 1) * dv)
        _hg_head(q_ref.at[:, ks], f_ref.at[:, ks], v_ref.at[:, vs], g_ref.at[:, vs], lbl_ref.at[:, ks], ng_ref,
                 y_ref.at[:, vs], state_ref.at[hh], b_ref.at[:, ks], k_ref.at[:, ks], tc=tc, layer=layer)


def _hg_head(q_ref, f_ref, v_ref, g_ref, lbl_ref, ng_ref, y_ref, state_ref, b_ref, k_ref, *, tc, layer):
    dk = q_ref.shape[1]
    nt = (((1,), (1,)), ((), ()))
    tn = (((0,), (0,)), ((), ()))

    lg = lbl_ref[...]
    e = jnp.exp(lg - jnp.max(lg, axis=0, keepdims=True))
    pl_sm = e * (1.0 / jnp.sum(e, axis=0, keepdims=True))
    lb = jnp.zeros((1, dk), F32)
    for r in range(1, layer + 1):
        lb = lb + pl_sm[r:r + 1, :]

    q = _silu(q_ref[...])
    fg = lb + (1.0 - lb) * _sigmoid(f_ref[...])
    kk = 1.0 - fg
    b = jnp.log2(fg)
    row = lax.broadcasted_iota(jnp.int32, (tc, 1), 0)
    rc = row & (HG_CHUNK - 1)
    d = 1
    while d < HG_CHUNK:
        b = b + jnp.where(rc >= d, pltpu.roll(b, d, 0), 0.0)
        d *= 2
    b_ref[...] = b
    k_ref[...] = kk

    nsub = tc // HG_SUB
    lane = lax.broadcasted_iota(jnp.int32, (dk, LANES), 1)
    acc = jnp.zeros((tc, LANES), F32)
    for s in range(HG_SUB):
        b_s = jnp.concatenate(
            [jnp.broadcast_to(b_ref[i * HG_SUB + s:i * HG_SUB + s + 1, :], (HG_SUB, dk)) for i in range(nsub)], axis=0)
        k_s = jnp.concatenate(
            [jnp.broadcast_to(k_ref[i * HG_SUB + s:i * HG_SUB + s + 1, :], (HG_SUB, dk)) for i in range(nsub)], axis=0)
        m_s = q * jnp.exp2(jnp.minimum(b - b_s, 0.0)) * k_s
        w_s = jnp.where(((lane & (HG_SUB - 1)) == s) & (lane < HG_CHUNK), 1.0, 0.0).astype(BF16)
        acc = acc + jnp.dot(m_s.astype(BF16), w_s, preferred_element_type=F32)

    col = lax.broadcasted_iota(jnp.int32, (HG_CHUNK, HG_CHUNK), 1)
    rw = lax.broadcasted_iota(jnp.int32, (HG_CHUNK, HG_CHUNK), 0)
    col_sub = lax.shift_right_logical(col, HG_SUB_SHIFT)
    rw_sub = lax.shift_right_logical(rw, HG_SUB_SHIFT)
    diag_mask = (col_sub == rw_sub) & (col <= rw)

    state = state_ref[...]
    nsc = HG_CHUNK // HG_SUB
    for c in range(tc // HG_CHUNK):
        r0 = c * HG_CHUNK
        bc = b[r0:r0 + HG_CHUNK, :]
        qc = q[r0:r0 + HG_CHUNK, :]
        kc = kk[r0:r0 + HG_CHUNK, :]
        vc = v_ref[r0:r0 + HG_CHUNK, :]
        o = lax.dot_general((qc * jnp.exp2(bc)).astype(BF16), state.astype(BF16), nt, preferred_element_type=F32)
        blocks = [jnp.zeros((HG_SUB, HG_CHUNK), F32)]
        for i in range(1, nsc):
            n_prev = i * HG_SUB
            r_i = bc[n_prev - 1:n_prev, :]
            q_i = (qc[n_prev:n_prev + HG_SUB, :] * jnp.exp2(bc[n_prev:n_prev + HG_SUB, :] - r_i)).astype(BF16)
            k_i = kc[0:n_prev, :] * jnp.exp2(r_i - bc[0:n_prev, :])
            k_i = jnp.concatenate([k_i, jnp.zeros((HG_CHUNK - n_prev, dk), F32)], axis=0).astype(BF16)
            blocks.append(lax.dot_general(q_i, k_i, nt, preferred_element_type=F32))
        att_off = jnp.concatenate(blocks, axis=0)
        att = jnp.where(diag_mask, acc[r0:r0 + HG_CHUNK, 0:HG_CHUNK],
                        jnp.where(col_sub < rw_sub, att_off, 0.0))
        o = o + jnp.dot(att.astype(BF16), vc, preferred_element_type=F32)
        b_last = bc[HG_CHUNK - 1:HG_CHUNK, :]
        k_dec = (kc * jnp.exp2(b_last - bc)).astype(BF16)
        state = state * jnp.exp2(b_last) + lax.dot_general(vc, k_dec, tn, preferred_element_type=F32)
        ms = jnp.mean(o * o, axis=-1, keepdims=True)
        on = o * lax.rsqrt(ms + NORM_EPS) * ng_ref[...]
        gc = g_ref[r0:r0 + HG_CHUNK, :].astype(F32)
        y_ref[r0:r0 + HG_CHUNK, :] = (on * _silu(gc)).astype(y_ref.dtype)
    state_ref[...] = state


def _hg_core(qf, vg, lb_logits, norm_gain, layer, tc=HG_TILE, heads=HG_HEADS_PER_STEP):
    bsz, s, _ = qf.shape
    dk = qf.shape[2] // (2 * HG_HEADS)
    dv = vg.shape[2] // (2 * HG_HEADS)
    nl = lb_logits.shape[0]
    groups = HG_HEADS // heads
    wk, wv = heads * dk, heads * dv
    return pl.pallas_call(
        functools.partial(_hg_kernel, tc=tc, layer=layer, heads=heads),
        out_shape=jax.ShapeDtypeStruct((bsz, s, HG_HEADS * dv), BF16),
        grid=(bsz, groups, s // tc),
        in_specs=[
            pl.BlockSpec((None, tc, wk), lambda b, h, i: (b, i, h)),
            pl.BlockSpec((None, tc, wk), lambda b, h, i: (b, i, groups + h)),
            pl.BlockSpec((None, tc, wv), lambda b, h, i: (b, i, h)),
            pl.BlockSpec((None, tc, wv), lambda b, h, i: (b, i, groups + h)),
            pl.BlockSpec((nl, wk), lambda b, h, i: (0, h)),
            pl.BlockSpec((1, dv), lambda b, h, i: (0, 0)),
        ],
        out_specs=pl.BlockSpec((None, tc, wv), lambda b, h, i: (b, i, h)),
        scratch_shapes=[pltpu.VMEM((heads, dv, dk), F32), pltpu.VMEM((tc, wk), F32), pltpu.VMEM((tc, wk), F32)],
        compiler_params=_cparams(("parallel", "parallel", "arbitrary")),
        name="hgrn2_core",
    )(qf, qf, vg, vg, lb_logits.astype(F32), norm_gain.reshape(1, dv).astype(F32))


def _hgrn2_mixer(h2d, bsz, seq, w_in, lb_logits, layer, norm_gain, w_out):
    val = w_out.shape[0]
    key = (w_in.shape[1] - 2 * val) // 2
    qf = _proj(h2d, w_in, F32, 0, 2 * key).reshape(bsz, seq, 2 * key)
    vg = _proj(h2d, w_in, BF16, 2 * key, 2 * val).reshape(bsz, seq, 2 * val)
    y = _hg_core(qf, vg, lb_logits, norm_gain, layer)
    return _proj(y.reshape(bsz * seq, val), w_out, BF16)


def kernel(x, pre_norm_gain, post_norm_gain, nsa_w_in, nsa_cmp_pe, nsa_cmp_w1, nsa_cmp_w2, nsa_w_out,
           rg_w_in, rg_conv_w, rg_conv_b, rg_gate_w, rg_gate_b, rg_lambda, rg_w_out,
           hg_w_in, hg_lb_logits, hg_norm_gain, hg_w_out):
    bsz, seq, d = x.shape
    depth = pre_norm_gain.shape[0]
    x2d = x.reshape(bsz * seq, d)
    h = _prenorm(x2d, pre_norm_gain[0])
    for i in range(depth):
        kind, j = i % 3, i // 3
        if kind == 0:
            y = _nsa_mixer(h, bsz, seq, j, nsa_w_in, nsa_cmp_pe, nsa_cmp_w1, nsa_cmp_w2, nsa_w_out)
        elif kind == 1:
            y = _rglru_mixer(h, bsz, seq, rg_w_in[j], rg_conv_w[j], rg_conv_b[j], rg_gate_w[j], rg_gate_b[j],
                             rg_lambda[j], rg_w_out[j])
        else:
            y = _hgrn2_mixer(h, bsz, seq, hg_w_in[j], hg_lb_logits, i, hg_norm_gain[j], hg_w_out[j])
        next_gain = pre_norm_gain[i + 1] if i + 1 < depth else None
        x2d, h = _postnorm_residual(x2d, y, post_norm_gain[i], next_gain)
    return x2d.reshape(bsz, seq, d)
```

```python
import functools
import math

import jax
import jax.numpy as jnp
from jax import lax
from jax.experimental import pallas as pl
from jax.experimental.pallas import tpu as pltpu

F32 = jnp.float32
BF16 = jnp.bfloat16

NORM_EPS = 1e-6
NEG_INF = -1e30
FORCE_SCORE = 1e6

LANES = 128
SUBLANES = 8
V7X_VMEM_LIMIT_BYTES = 56 * 1024 * 1024

NSA_HEADS = 32
NSA_GROUPS = 4
NSA_HPG = NSA_HEADS // NSA_GROUPS
HEAD_DIM = 128
CMP_LEN = 32
CMP_STRIDE = 16
SEL_LEN = 64
SEL_SHIFT = 6
SEL_TOPK = 16
WINDOW = 512
ALIBI_MAX_EXP = 8.0
NSA_MASK_BIG = 2.0 ** 40
NSA_STACK = 4

RG_BLOCKS = 16
RG_CONV = 4
RG_C = 8.0

HG_HEADS = 32
HG_CHUNK = 64
HG_SUB = 8
HG_SUB_SHIFT = 3


NORM_ROWS = 256
PROJ_ROWS = 1024
PROJ_COLS = 512
PROJ_ROW_BUFFERS = 3
NSA_QUERY_TILE = 256
RG_TILE = 1024
HG_TILE = 512
HG_HEADS_PER_STEP = 2


def _cparams(sem):
    return pltpu.CompilerParams(dimension_semantics=sem, vmem_limit_bytes=V7X_VMEM_LIMIT_BYTES)


def _sigmoid(x):
    return 0.5 * jnp.tanh(0.5 * x) + 0.5


def _silu(x):
    return x * _sigmoid(x)


def _prenorm_kernel(x_ref, g_ref, o_ref):
    x = x_ref[...]
    ms = jnp.mean(x * x, axis=-1, keepdims=True)
    o_ref[...] = (x * lax.rsqrt(ms + NORM_EPS) * g_ref[...]).astype(o_ref.dtype)


def _prenorm(x2d, gain, tm=NORM_ROWS):
    t, d = x2d.shape
    return pl.pallas_call(
        _prenorm_kernel,
        out_shape=jax.ShapeDtypeStruct((t, d), BF16),
        grid=(t // tm,),
        in_specs=[pl.BlockSpec((tm, d), lambda i: (i, 0)), pl.BlockSpec((1, d), lambda i: (0, 0))],
        out_specs=pl.BlockSpec((tm, d), lambda i: (i, 0)),
        compiler_params=_cparams(("parallel",)),
        name="prenorm",
    )(x2d, gain.reshape(1, d).astype(F32))


def _postnorm_kernel(x_ref, y_ref, g_ref, *rest):
    y = y_ref[...].astype(F32)
    ms = jnp.mean(y * y, axis=-1, keepdims=True)
    xn = x_ref[...] + y * lax.rsqrt(ms + NORM_EPS) * g_ref[...]
    if len(rest) == 1:
        rest[0][...] = xn
    else:
        gn_ref, o_ref, h_ref = rest
        o_ref[...] = xn
        ms2 = jnp.mean(xn * xn, axis=-1, keepdims=True)
        h_ref[...] = (xn * lax.rsqrt(ms2 + NORM_EPS) * gn_ref[...]).astype(h_ref.dtype)


def _postnorm_residual(x2d, y2d, gain, next_gain=None, tm=NORM_ROWS):
    t, d = x2d.shape
    row = pl.BlockSpec((tm, d), lambda i: (i, 0))
    vec = pl.BlockSpec((1, d), lambda i: (0, 0))
    operands = [x2d, y2d, gain.reshape(1, d).astype(F32)]
    in_specs = [row, row, vec]
    out_shape = jax.ShapeDtypeStruct((t, d), F32)
    out_specs = row
    if next_gain is not None:
        operands.append(next_gain.reshape(1, d).astype(F32))
        in_specs.append(vec)
        out_shape = (out_shape, jax.ShapeDtypeStruct((t, d), BF16))
        out_specs = (row, row)
    res = pl.pallas_call(
        _postnorm_kernel,
        out_shape=out_shape,
        grid=(t // tm,),
        in_specs=in_specs,
        out_specs=out_specs,
        compiler_params=_cparams(("parallel",)),
        name="postnorm_residual",
    )(*operands)
    return res if next_gain is not None else (res, None)


def _proj_kernel(a_ref, w_ref, *rest, shift, transposed, tm):
    if shift:
        wn_ref, o_ref, wb_ref = rest
    else:
        o_ref, wb_ref = rest

    if shift:
        tn = wb_ref.shape[0]
        wb_ref[0:tn - shift, :] = w_ref[shift:tn, :].astype(BF16)
        wb_ref[tn - shift:tn, :] = wn_ref[0:shift, :].astype(BF16)
    else:
        tn = wb_ref.shape[0] if transposed else wb_ref.shape[1]
        wb_ref[...] = w_ref[...].astype(BF16)

    def rows_step(x_ref, y_ref):
        if transposed:
            acc = lax.dot_general(x_ref[...], wb_ref[...], (((1,), (1,)), ((), ())), preferred_element_type=F32)
        else:
            acc = jnp.dot(x_ref[...], wb_ref[...], preferred_element_type=F32)
        y_ref[...] = acc.astype(y_ref.dtype)

    m, kdim = a_ref.shape
    col = pl.multiple_of(pl.program_id(0) * tn, tn)
    pltpu.emit_pipeline(
        rows_step,
        grid=(m // tm,),
        in_specs=[pl.BlockSpec((tm, kdim), lambda i: (i, 0), pipeline_mode=pl.Buffered(PROJ_ROW_BUFFERS))],
        out_specs=[pl.BlockSpec((tm, tn), lambda i: (i, 0))],
    )(a_ref, o_ref.at[:, pl.ds(col, tn)])


def _proj(a, w, out_dtype, col0=0, ncols=None, layer=0, shift=0, transposed=False, tm=PROJ_ROWS, tn=PROJ_COLS):
    if w.ndim == 2:
        w = w.reshape((1,) + w.shape)
    m, kdim = a.shape
    k_axis, n_axis = (2, 1) if transposed else (1, 2)
    ncols = w.shape[n_axis] - col0 if ncols is None else ncols
    tm, tn = min(tm, m), min(tn, ncols)
    assert m % tm == 0 and ncols % tn == 0 and col0 % tn == 0 and w.shape[k_axis] == kdim
    assert shift == 0 or (transposed and 0 < shift < LANES and shift % (2 * SUBLANES) == 0 and tn % LANES == 0)
    cb0 = col0 // tn
    if transposed:
        w_block = (None, tn, kdim)
        w_spec = pl.BlockSpec(w_block, lambda j: (layer, cb0 + j, 0))
    else:
        w_block = (None, kdim, tn)
        w_spec = pl.BlockSpec(w_block, lambda j: (layer, 0, cb0 + j))
    in_specs = [pl.BlockSpec(memory_space=pl.ANY), w_spec]
    operands = [a, w]
    if shift:
        lane_tiles = tn // LANES
        in_specs.append(pl.BlockSpec((None, LANES, kdim), lambda j: (layer, (cb0 + j + 1) * lane_tiles, 0)))
        operands.append(w)
    return pl.pallas_call(
        functools.partial(_proj_kernel, shift=shift, transposed=transposed, tm=tm),
        out_shape=jax.ShapeDtypeStruct((m, ncols), out_dtype),
        grid=(ncols // tn,),
        in_specs=in_specs,
        out_specs=pl.BlockSpec(memory_space=pl.ANY),
        scratch_shapes=[pltpu.VMEM(w_block[1:], BF16)],
        compiler_params=_cparams(("arbitrary",)),
        name="projection",
    )(*operands)


def _gelu_tanh(x):
    return 0.5 * x * (1.0 + jnp.tanh(math.sqrt(2.0 / math.pi) * (x + 0.044715 * (x * x * x))))


def _compress_kernel(x_ref, pe_ref, w1_ref, w2_ref, o_ref, xf_ref):
    s = x_ref.shape[0]
    nch = s // CMP_STRIDE
    xf_ref[...] = x_ref[...].astype(F32)
    acc_lo = jnp.zeros((nch, HEAD_DIM), F32)
    acc_hi = jnp.zeros((nch, HEAD_DIM), F32)
    for l in range(CMP_STRIDE):
        xl = xf_ref[pl.ds(l, nch, stride=CMP_STRIDE), :]
        lo_in = (xl + pe_ref[l:l + 1, :]).astype(BF16)
        hi_in = (xl + pe_ref[CMP_STRIDE + l:CMP_STRIDE + l + 1, :]).astype(BF16)
        acc_lo += jnp.dot(lo_in, w1_ref[l * HEAD_DIM:(l + 1) * HEAD_DIM, :], preferred_element_type=F32)
        acc_hi += jnp.dot(hi_in, w1_ref[(CMP_STRIDE + l) * HEAD_DIM:(CMP_STRIDE + l + 1) * HEAD_DIM, :],
                          preferred_element_type=F32)
    pre = acc_lo + pltpu.roll(acc_hi, nch - 1, 0)
    hid = _gelu_tanh(pre).astype(BF16)
    o_ref[...] = jnp.dot(hid, w2_ref[...], preferred_element_type=F32).astype(o_ref.dtype)


def _nsa_compress(qkv, pe, w1, w2):
    bsz, s, _ = qkv.shape
    nch = s // CMP_STRIDE
    kv_block0 = NSA_HEADS
    return pl.pallas_call(
        _compress_kernel,
        out_shape=jax.ShapeDtypeStruct((bsz, 2, NSA_GROUPS, nch, HEAD_DIM), BF16),
        grid=(bsz, 2, NSA_GROUPS),
        in_specs=[
            pl.BlockSpec((None, s, HEAD_DIM), lambda b, w, g: (b, 0, kv_block0 + w * NSA_GROUPS + g)),
            pl.BlockSpec((None, CMP_LEN, HEAD_DIM), lambda b, w, g: (w, 0, 0)),
            pl.BlockSpec((None, CMP_LEN * HEAD_DIM, HEAD_DIM), lambda b, w, g: (w, 0, 0)),
            pl.BlockSpec((None, HEAD_DIM, HEAD_DIM), lambda b, w, g: (w, 0, 0)),
        ],
        out_specs=pl.BlockSpec((None, None, None, nch, HEAD_DIM), lambda b, w, g: (b, w, g, 0, 0)),
        scratch_shapes=[pltpu.VMEM((s, HEAD_DIM), F32)],
        compiler_params=_cparams(("parallel", "parallel", "parallel")),
        name="nsa_compress",
    )(qkv, pe.astype(F32), w1.astype(BF16), w2.astype(BF16))


def _split3(x):
    hi = x.astype(BF16)
    r1 = x - hi.astype(F32)
    mid = r1.astype(BF16)
    lo = (r1 - mid.astype(F32)).astype(BF16)
    return hi, mid, lo


def _nsa_attn_kernel(slx_ref, kx_ref, cx_ref, q_ref, kcmp_ref, vcmp_ref, ks_ref, vs_ref, kw_ref, vw_ref,
                     gl_ref, z_ref, y_ref, ksx_ref, kwx_ref, kcx_ref, qx_ref, oc_ref, *, tq, seq):
    qi = pl.program_id(2)
    t0 = qi * tq
    scale = HEAD_DIM ** -0.5
    n_cmp_pad = kcmp_ref.shape[0]
    n_sel = seq // SEL_LEN
    rows = NSA_HPG * tq
    nt = (((1,), (1,)), ((), ()))

    @pl.when(qi == 0)
    def _():
        kx = kx_ref[...]
        lane_k = lax.broadcasted_iota(jnp.int32, kx.shape, 1)
        ksx_ref[:, 0:HEAD_DIM] = ks_ref[...]
        ksx_ref[:, HEAD_DIM:] = kx
        kwx_ref[:, 0:HEAD_DIM] = kw_ref[...]
        kwx_ref[:, HEAD_DIM:] = jnp.where(lane_k >= n_sel, kx, jnp.zeros_like(kx))

        kcx_ref[:, 0:HEAD_DIM] = kcmp_ref[...]
        kcx_ref[:, HEAD_DIM:] = cx_ref[...]

    tpos_i = t0 + lax.broadcasted_iota(jnp.int32, (tq, 1), 0)
    exp2_scale = scale * math.log2(math.e)

    for j in range(NSA_HPG):
        qx_ref[j * tq:(j + 1) * tq, 0:HEAD_DIM] = q_ref[:, j * HEAD_DIM:(j + 1) * HEAD_DIM]
        qx_ref[j * tq:(j + 1) * tq, HEAD_DIM:] = jnp.broadcast_to(slx_ref[j:j + 1, :], (tq, LANES)).astype(BF16)

    t_rows = t0 + (lax.broadcasted_iota(jnp.int32, (rows, 1), 0) & (tq - 1))
    cmp_end = lax.broadcasted_iota(jnp.int32, (1, n_cmp_pad), 1) * CMP_STRIDE + (CMP_LEN - 1)
    valid_c = t_rows >= cmp_end
    s = lax.dot_general(qx_ref[...], kcx_ref[...], nt, preferred_element_type=F32)
    s = jnp.where(valid_c, s, NEG_INF)
    m = jnp.max(s, axis=-1, keepdims=True)
    p = jnp.exp2((s - m) * exp2_scale)
    l = jnp.sum(p, axis=-1, keepdims=True)
    p = jnp.where(valid_c, p * (1.0 / l), 0.0)
    oc_ref[...] = jnp.dot(p.astype(BF16), vcmp_ref[...], preferred_element_type=F32)
    pg = p[0:tq, :]
    for j in range(1, NSA_HPG):
        pg = pg + p[j * tq:(j + 1) * tq, :]

    n_idx = lax.broadcasted_iota(jnp.int32, (n_cmp_pad, LANES), 0)
    j_idx = lax.broadcasted_iota(jnp.int32, (n_cmp_pad, LANES), 1)
    dd = n_idx - (SEL_LEN // CMP_STRIDE) * j_idx + (CMP_LEN // CMP_STRIDE - 1)
    pool = jnp.where((dd == 0) | (dd == 4), 1.0, jnp.where((dd >= 1) & (dd <= 3), 2.0, 0.0)).astype(BF16)
    p_slc = jnp.zeros((tq, LANES), F32)
    for part in _split3(pg):
        p_slc = p_slc + jnp.dot(part, pool, preferred_element_type=F32)

    blk = lax.broadcasted_iota(jnp.int32, (1, LANES), 1)
    cur = lax.shift_right_logical(tpos_i, SEL_SHIFT)
    forced = (blk == 0) | (blk == cur) | (blk == cur - 1)
    future = blk > cur
    score = jnp.where(forced, FORCE_SCORE, jnp.where(future, -1.0, p_slc))
    score = jnp.where(blk < n_sel, score, -2.0)
    score_t = score.T[0:n_sel, :]
    blk_t = lax.broadcasted_iota(jnp.int32, (n_sel, 1), 0)
    rank = jnp.zeros((n_sel, tq), F32)
    for jp in range(n_sel):
        other = score_t[jp:jp + 1, :]
        ahead = (other > score_t) | ((other == score_t) & (blk_t > jp))
        rank = rank + jnp.where(ahead, 1.0, 0.0)
    pen_t = jnp.where(rank < float(min(SEL_TOPK, n_sel)), 0.0, -NSA_MASK_BIG)
    penalty = jnp.concatenate([pen_t, jnp.zeros((LANES - n_sel, tq), F32)], axis=0).T

    for j in range(NSA_HPG):
        qx_ref[j * tq:(j + 1) * tq, HEAD_DIM:] = (penalty + slx_ref[j:j + 1, :]).astype(BF16)

    hs = NSA_STACK
    t_loc = lax.broadcasted_iota(jnp.int32, (hs * tq, 1), 0) & (tq - 1)
    k_loc = lax.broadcasted_iota(jnp.int32, (1, tq), 1)
    causal = k_loc <= t_loc
    far_ok = k_loc > t_loc
    n_back = WINDOW // tq
    gates = _sigmoid(gl_ref[...].astype(F32))
    lane_g = lax.broadcasted_iota(jnp.int32, (1, LANES), 1)
    head0 = pl.program_id(1) * NSA_HPG

    def attend(qx, pieces):
        ss = []
        for kx_p, _, mask in pieces:
            s = lax.dot_general(qx, kx_p, nt, preferred_element_type=F32)
            ss.append(s if mask is None else jnp.where(mask, s, NEG_INF))
        m = functools.reduce(jnp.maximum, [jnp.max(s, axis=-1, keepdims=True) for s in ss])
        l = 0.0
        o = 0.0
        for s, (_, v_p, _) in zip(ss, pieces):
            p = jnp.exp2((s - m) * exp2_scale)
            l = l + jnp.sum(p, axis=-1, keepdims=True)
            o = o + jnp.dot(p.astype(BF16), v_p, preferred_element_type=F32)
        return o * (1.0 / l)

    def variant(n):
        d0 = (n - 1) * tq

        def head_body(jg, carry):
            r0 = pl.multiple_of(jg * (hs * tq), hs * tq)
            qx = qx_ref[pl.ds(r0, hs * tq), :]
            sel_pieces = [(ksx_ref[d0:d0 + tq, :], vs_ref[d0:d0 + tq, :], causal)]
            if n > 1:
                sel_pieces.append((ksx_ref[0:d0, :], vs_ref[0:d0, :], None))
            o_slc = attend(qx, sel_pieces)
            win_pieces = [(kwx_ref[d0:d0 + tq, :], vw_ref[d0:d0 + tq, :], causal)]
            for w in range(1, min(n - 1, n_back) + 1):
                k0 = d0 - w * tq
                win_pieces.append((kwx_ref[k0:k0 + tq, :], vw_ref[k0:k0 + tq, :], far_ok if w == n_back else None))
            o_win = attend(qx, win_pieces)

            for jj in range(hs):
                j = jg * hs + jj
                rows_j = slice(jj * tq, (jj + 1) * tq)
                c0 = pl.multiple_of(j * HEAD_DIM, HEAD_DIM)

                def gate(branch):
                    pick = lane_g == branch * NSA_HEADS + head0 + j
                    return jnp.sum(jnp.where(pick, gates, 0.0), axis=-1, keepdims=True)

                o = (gate(0) * oc_ref[pl.ds(r0 + jj * tq, tq), :] + gate(1) * o_slc[rows_j, :]
                     + gate(2) * o_win[rows_j, :])
                zj = z_ref[:, pl.ds(c0, HEAD_DIM)].astype(F32)
                y_ref[:, pl.ds(c0, HEAD_DIM)] = (o * _silu(zj)).astype(y_ref.dtype)
            return carry

        lax.fori_loop(0, NSA_HPG // hs, head_body, 0)

    for n in range(1, seq // tq + 1):
        pl.when(qi == n - 1)(functools.partial(variant, n))


def _alibi_slopes():
    return 2.0 ** (-ALIBI_MAX_EXP * jnp.arange(1, NSA_HEADS + 1, dtype=F32) / NSA_HEADS)


def _nsa_tables(seq, n_cmp_pad):
    slopes = _alibi_slopes()
    n_sel = seq // SEL_LEN
    parts = []
    rest = slopes * (HEAD_DIM ** 0.5)
    for _ in range(3):
        part = rest.astype(BF16).astype(F32)
        parts.append(part)
        rest = rest - part
    slx = jnp.zeros((NSA_HEADS, LANES), F32)
    for c, part in enumerate(parts + parts):
        slx = slx.at[:, n_sel + c].set(part)
    slx = slx.reshape(NSA_GROUPS, NSA_HPG, LANES)
    lane = jnp.arange(LANES, dtype=jnp.int32)[None, :]

    def position_lanes(pos, base):
        hi = ((pos // SEL_LEN) * SEL_LEN).astype(F32)[:, None]
        lo = (pos % SEL_LEN).astype(F32)[:, None]
        out = jnp.where((lane >= n_sel) & (lane < n_sel + 3), hi, base)
        return jnp.where((lane >= n_sel + 3) & (lane < n_sel + 6), lo, out).astype(BF16)

    key = jnp.arange(seq, dtype=jnp.int32)
    kx = position_lanes(key, jnp.where(lane == (key // SEL_LEN)[:, None], 1.0, 0.0))
    cmp_end = jnp.arange(n_cmp_pad, dtype=jnp.int32) * CMP_STRIDE + (CMP_LEN - 1)
    cx = position_lanes(cmp_end, jnp.zeros((n_cmp_pad, LANES), F32))
    return slx, kx, cx


def _nsa_attention(qkv, kv_cmp, gl, z, tq=NSA_QUERY_TILE):
    bsz, s, _ = qkv.shape
    gw = NSA_HPG * HEAD_DIM
    ncp = kv_cmp.shape[3]
    kvb = NSA_HEADS
    rows = NSA_HPG * tq
    assert WINDOW % tq == 0 and s // SEL_LEN + 6 <= LANES and tq % SEL_LEN == 0
    slx, kx, cx = _nsa_tables(s, ncp)

    def kv_spec(which):
        return pl.BlockSpec((None, s, HEAD_DIM), lambda b, g, i: (b, 0, kvb + which * NSA_GROUPS + g))

    return pl.pallas_call(
        functools.partial(_nsa_attn_kernel, tq=tq, seq=s),
        out_shape=jax.ShapeDtypeStruct((bsz, s, NSA_HEADS * HEAD_DIM), BF16),
        grid=(bsz, NSA_GROUPS, s // tq),
        in_specs=[
            pl.BlockSpec((None, NSA_HPG, LANES), lambda b, g, i: (g, 0, 0)),
            pl.BlockSpec((s, LANES), lambda b, g, i: (0, 0)),
            pl.BlockSpec((ncp, LANES), lambda b, g, i: (0, 0)),
            pl.BlockSpec((None, tq, gw), lambda b, g, i: (b, i, g)),
            pl.BlockSpec((None, None, None, ncp, HEAD_DIM), lambda b, g, i: (b, 0, g, 0, 0)),
            pl.BlockSpec((None, None, None, ncp, HEAD_DIM), lambda b, g, i: (b, 1, g, 0, 0)),
            kv_spec(2), kv_spec(3), kv_spec(4), kv_spec(5),
            pl.BlockSpec((None, tq, LANES), lambda b, g, i: (b, i, 0)),
            pl.BlockSpec((None, tq, gw), lambda b, g, i: (b, i, g)),
        ],
        out_specs=pl.BlockSpec((None, tq, gw), lambda b, g, i: (b, i, g)),
        scratch_shapes=[
            pltpu.VMEM((s, 2 * HEAD_DIM), BF16),
            pltpu.VMEM((s, 2 * HEAD_DIM), BF16),
            pltpu.VMEM((ncp, 2 * HEAD_DIM), BF16),
            pltpu.VMEM((rows, 2 * HEAD_DIM), BF16),
            pltpu.VMEM((rows, HEAD_DIM), F32),
        ],
        compiler_params=_cparams(("parallel", "parallel", "arbitrary")),
        name="nsa_attention",
    )(slx, kx, cx, qkv, kv_cmp, kv_cmp, qkv, qkv, qkv, qkv, gl, z)


def _nsa_mixer(h2d, bsz, seq, layer, w_in, cmp_pe, cmp_w1, cmp_w2, w_out):
    inner = NSA_HEADS * HEAD_DIM
    kv = NSA_GROUPS * HEAD_DIM
    n_qkv = inner + 6 * kv
    n_gl = 3 * NSA_HEADS
    assert n_gl < LANES and n_qkv % LANES == 0
    w_in_t = jnp.swapaxes(w_in, 1, 2)
    qkv = _proj(h2d, w_in_t, BF16, 0, n_qkv, layer, transposed=True).reshape(bsz, seq, n_qkv)
    gl = _proj(h2d, w_in_t, F32, n_qkv, LANES, layer, transposed=True).reshape(bsz, seq, LANES)
    z = _proj(h2d, w_in_t, BF16, n_qkv, inner, layer, shift=n_gl, transposed=True).reshape(bsz, seq, inner)
    kv_cmp = _nsa_compress(qkv, cmp_pe[layer], cmp_w1[layer], cmp_w2[layer])
    y = _nsa_attention(qkv, kv_cmp, gl, z)
    return _proj(y.reshape(bsz * seq, inner), w_out, BF16, layer=layer)


def _rg_kernel(xb_ref, z_ref, cw_ref, cb_ref, gw_ref, gb_ref, lam_ref, y_ref, xpad_ref, h_ref, *, ts):
    si = pl.program_id(2)
    halo = SUBLANES

    @pl.when(si == 0)
    def _():
        xpad_ref[0:halo, :] = jnp.zeros((halo, xpad_ref.shape[1]), F32)
        h_ref[...] = jnp.zeros_like(h_ref)

    x = xb_ref[...]
    xpad_ref[halo:halo + ts, :] = x
    xc = cb_ref[...] + cw_ref[RG_CONV - 1:RG_CONV, :] * x
    for k in range(RG_CONV - 1):
        shift = RG_CONV - 1 - k
        xc = xc + cw_ref[k:k + 1, :] * xpad_ref[halo - shift:halo - shift + ts, :]
    xpad_ref[0:halo, :] = x[ts - halo:ts, :]

    xcb = xc.astype(BF16)
    gate_i = _sigmoid(jnp.dot(xcb, gw_ref[0], preferred_element_type=F32) + gb_ref[0])
    gate_r = _sigmoid(jnp.dot(xcb, gw_ref[1], preferred_element_type=F32) + gb_ref[1])
    nl = -lam_ref[...]
    softplus = jnp.maximum(nl, 0.0) + jnp.log(1.0 + jnp.exp(-jnp.abs(nl)))
    log_a = (-RG_C) * gate_r * softplus
    a = jnp.exp(log_a)
    one_m_a2 = 1.0 - a * a
    mult = jnp.where(one_m_a2 > 0.0, one_m_a2 * lax.rsqrt(one_m_a2), 0.0)
    row = lax.broadcasted_iota(jnp.int32, (ts, 1), 0)
    mult = jnp.where((row + si * ts) == 0, 1.0, mult)
    u = mult * gate_i * xc

    row_in_group = row & (SUBLANES - 1)
    d = 1
    while d < SUBLANES:
        keep = row_in_group >= d
        a_sh = pltpu.roll(a, d, 0)
        u_sh = pltpu.roll(u, d, 0)
        u = jnp.where(keep, a * u_sh + u, u)
        a = jnp.where(keep, a * a_sh, a)
        d *= 2
    carry = h_ref[...]
    groups = []
    for g in range(ts // SUBLANES):
        rows_g = slice(g * SUBLANES, (g + 1) * SUBLANES)
        h_g = a[rows_g, :] * carry + u[rows_g, :]
        groups.append(h_g)
        carry = h_g[SUBLANES - 1:SUBLANES, :]
    h_ref[...] = carry
    hs = jnp.concatenate(groups, axis=0)
    y_ref[...] = (hs * _silu(z_ref[...].astype(F32))).astype(y_ref.dtype)


def _rg_core(xb, z, conv_w, conv_b, gate_w, gate_b, lam, ts=RG_TILE):
    bsz, s, w = xb.shape
    cb = w // RG_BLOCKS
    return pl.pallas_call(
        functools.partial(_rg_kernel, ts=ts),
        out_shape=jax.ShapeDtypeStruct((bsz, s, w), BF16),
        grid=(bsz, RG_BLOCKS, s // ts),
        in_specs=[
            pl.BlockSpec((None, ts, cb), lambda b, n, i: (b, i, n)),
            pl.BlockSpec((None, ts, cb), lambda b, n, i: (b, i, n)),
            pl.BlockSpec((RG_CONV, cb), lambda b, n, i: (0, n)),
            pl.BlockSpec((1, cb), lambda b, n, i: (0, n)),
            pl.BlockSpec((2, None, cb, cb), lambda b, n, i: (0, n, 0, 0)),
            pl.BlockSpec((2, None, 1, cb), lambda b, n, i: (0, n, 0, 0)),
            pl.BlockSpec((1, cb), lambda b, n, i: (0, n)),
        ],
        out_specs=pl.BlockSpec((None, ts, cb), lambda b, n, i: (b, i, n)),
        scratch_shapes=[pltpu.VMEM((ts + SUBLANES, cb), F32), pltpu.VMEM((1, cb), F32)],
        compiler_params=_cparams(("parallel", "parallel", "arbitrary")),
        name="rglru_core",
    )(xb, z, conv_w.astype(F32), conv_b.reshape(1, w).astype(F32), gate_w.astype(BF16),
      gate_b.reshape(2, RG_BLOCKS, 1, cb).astype(F32), lam.reshape(1, w).astype(F32))


def _rglru_mixer(h2d, bsz, seq, w_in, conv_w, conv_b, gate_w, gate_b, lam, w_out):
    width = w_out.shape[0]
    xb = _proj(h2d, w_in, F32, 0, width).reshape(bsz, seq, width)
    z = _proj(h2d, w_in, BF16, width, width).reshape(bsz, seq, width)
    y = _rg_core(xb, z, conv_w, conv_b, gate_w, gate_b, lam)
    return _proj(y.reshape(bsz * seq, width), w_out, BF16)


def _hg_kernel(q_ref, f_ref, v_ref, g_ref, lbl_ref, ng_ref, y_ref, state_ref, b_ref, k_ref, *, tc, layer, heads):
    @pl.when(pl.program_id(2) == 0)
    def _():
        state_ref[...] = jnp.zeros_like(state_ref)

    dk = q_ref.shape[1] // heads
    dv = v_ref.shape[1] // heads
    for hh in range(heads):
        ks = slice(hh * dk, (hh + 1) * dk)
        vs = slice(hh * dv, (hh + 1) * dv)
        _hg_head(q_ref.at[:, ks], f_ref.at[:, ks], v_ref.at[:, vs], g_ref.at[:, vs], lbl_ref.at[:, ks], ng_ref,
                 y_ref.at[:, vs], state_ref.at[hh], b_ref.at[:, ks], k_ref.at[:, ks], tc=tc, layer=layer)


def _hg_head(q_ref, f_ref, v_ref, g_ref, lbl_ref, ng_ref, y_ref, state_ref, b_ref, k_ref, *, tc, layer):
    dk = q_ref.shape[1]
    nt = (((1,), (1,)), ((), ()))
    tn = (((0,), (0,)), ((), ()))

    lg = lbl_ref[...]
    e = jnp.exp(lg - jnp.max(lg, axis=0, keepdims=True))
    pl_sm = e * (1.0 / jnp.sum(e, axis=0, keepdims=True))
    lb = jnp.zeros((1, dk), F32)
    for r in range(1, layer + 1):
        lb = lb + pl_sm[r:r + 1, :]

    q = _silu(q_ref[...])
    fg = lb + (1.0 - lb) * _sigmoid(f_ref[...])
    kk = 1.0 - fg
    b = jnp.log2(fg)
    row = lax.broadcasted_iota(jnp.int32, (tc, 1), 0)
    rc = row & (HG_CHUNK - 1)
    d = 1
    while d < HG_CHUNK:
        b = b + jnp.where(rc >= d, pltpu.roll(b, d, 0), 0.0)
        d *= 2
    b_ref[...] = b
    k_ref[...] = kk

    nsub = tc // HG_SUB
    lane = lax.broadcasted_iota(jnp.int32, (dk, LANES), 1)
    acc = jnp.zeros((tc, LANES), F32)
    for s in range(HG_SUB):
        b_s = jnp.concatenate(
            [jnp.broadcast_to(b_ref[i * HG_SUB + s:i * HG_SUB + s + 1, :], (HG_SUB, dk)) for i in range(nsub)], axis=0)
        k_s = jnp.concatenate(
            [jnp.broadcast_to(k_ref[i * HG_SUB + s:i * HG_SUB + s + 1, :], (HG_SUB, dk)) for i in range(nsub)], axis=0)
        m_s = q * jnp.exp2(jnp.minimum(b - b_s, 0.0)) * k_s
        w_s = jnp.where(((lane & (HG_SUB - 1)) == s) & (lane < HG_CHUNK), 1.0, 0.0).astype(BF16)
        acc = acc + jnp.dot(m_s.astype(BF16), w_s, preferred_element_type=F32)

    col = lax.broadcasted_iota(jnp.int32, (HG_CHUNK, HG_CHUNK), 1)
    rw = lax.broadcasted_iota(jnp.int32, (HG_CHUNK, HG_CHUNK), 0)
    col_sub = lax.shift_right_logical(col, HG_SUB_SHIFT)
    rw_sub = lax.shift_right_logical(rw, HG_SUB_SHIFT)
    diag_mask = (col_sub == rw_sub) & (col <= rw)

    state = state_ref[...]
    nsc = HG_CHUNK // HG_SUB
    for c in range(tc // HG_CHUNK):
        r0 = c * HG_CHUNK
        bc = b[r0:r0 + HG_CHUNK, :]
        qc = q[r0:r0 + HG_CHUNK, :]
        kc = kk[r0:r0 + HG_CHUNK, :]
        vc = v_ref[r0:r0 + HG_CHUNK, :]
        o = lax.dot_general((qc * jnp.exp2(bc)).astype(BF16), state.astype(BF16), nt, preferred_element_type=F32)
        blocks = [jnp.zeros((HG_SUB, HG_CHUNK), F32)]
        for i in range(1, nsc):
            n_prev = i * HG_SUB
            r_i = bc[n_prev - 1:n_prev, :]
            q_i = (qc[n_prev:n_prev + HG_SUB, :] * jnp.exp2(bc[n_prev:n_prev + HG_SUB, :] - r_i)).astype(BF16)
            k_i = kc[0:n_prev, :] * jnp.exp2(r_i - bc[0:n_prev, :])
            k_i = jnp.concatenate([k_i, jnp.zeros((HG_CHUNK - n_prev, dk), F32)], axis=0).astype(BF16)
            blocks.append(lax.dot_general(q_i, k_i, nt, preferred_element_type=F32))
        att_off = jnp.concatenate(blocks, axis=0)
        att = jnp.where(diag_mask, acc[r0:r0 + HG_CHUNK, 0:HG_CHUNK],
                        jnp.where(col_sub < rw_sub, att_off, 0.0))
        o = o + jnp.dot(att.astype(BF16), vc, preferred_element_type=F32)
        b_last = bc[HG_CHUNK - 1:HG_CHUNK, :]
        k_dec = (kc * jnp.exp2(b_last - bc)).astype(BF16)
        state = state * jnp.exp2(b_last) + lax.dot_general(vc, k_dec, tn, preferred_element_type=F32)
        ms = jnp.mean(o * o, axis=-1, keepdims=True)
        on = o * lax.rsqrt(ms + NORM_EPS) * ng_ref[...]
        gc = g_ref[r0:r0 + HG_CHUNK, :].astype(F32)
        y_ref[r0:r0 + HG_CHUNK, :] = (on * _silu(gc)).astype(y_ref.dtype)
    state_ref[...] = state


def _hg_core(qf, vg, lb_logits, norm_gain, layer, tc=HG_TILE, heads=HG_HEADS_PER_STEP):
    bsz, s, _ = qf.shape
    dk = qf.shape[2] // (2 * HG_HEADS)
    dv = vg.shape[2] // (2 * HG_HEADS)
    nl = lb_logits.shape[0]
    groups = HG_HEADS // heads
    wk, wv = heads * dk, heads * dv
    return pl.pallas_call(
        functools.partial(_hg_kernel, tc=tc, layer=layer, heads=heads),
        out_shape=jax.ShapeDtypeStruct((bsz, s, HG_HEADS * dv), BF16),
        grid=(bsz, groups, s // tc),
        in_specs=[
            pl.BlockSpec((None, tc, wk), lambda b, h, i: (b, i, h)),
            pl.BlockSpec((None, tc, wk), lambda b, h, i: (b, i, groups + h)),
            pl.BlockSpec((None, tc, wv), lambda b, h, i: (b, i, h)),
            pl.BlockSpec((None, tc, wv), lambda b, h, i: (b, i, groups + h)),
            pl.BlockSpec((nl, wk), lambda b, h, i: (0, h)),
            pl.BlockSpec((1, dv), lambda b, h, i: (0, 0)),
        ],
        out_specs=pl.BlockSpec((None, tc, wv), lambda b, h, i: (b, i, h)),
        scratch_shapes=[pltpu.VMEM((heads, dv, dk), F32), pltpu.VMEM((tc, wk), F32), pltpu.VMEM((tc, wk), F32)],
        compiler_params=_cparams(("parallel", "parallel", "arbitrary")),
        name="hgrn2_core",
    )(qf, qf, vg, vg, lb_logits.astype(F32), norm_gain.reshape(1, dv).astype(F32))


def _hgrn2_mixer(h2d, bsz, seq, w_in, lb_logits, layer, norm_gain, w_out):
    val = w_out.shape[0]
    key = (w_in.shape[1] - 2 * val) // 2
    qf = _proj(h2d, w_in, F32, 0, 2 * key).reshape(bsz, seq, 2 * key)
    vg = _proj(h2d, w_in, BF16, 2 * key, 2 * val).reshape(bsz, seq, 2 * val)
    y = _hg_core(qf, vg, lb_logits, norm_gain, layer)
    return _proj(y.reshape(bsz * seq, val), w_out, BF16)


def kernel(x, pre_norm_gain, post_norm_gain, nsa_w_in, nsa_cmp_pe, nsa_cmp_w1, nsa_cmp_w2, nsa_w_out,
           rg_w_in, rg_conv_w, rg_conv_b, rg_gate_w, rg_gate_b, rg_lambda, rg_w_out,
           hg_w_in, hg_lb_logits, hg_norm_gain, hg_w_out):
    bsz, seq, d = x.shape
    depth = pre_norm_gain.shape[0]
    x2d = x.reshape(bsz * seq, d)
    h = _prenorm(x2d, pre_norm_gain[0])
    for i in range(depth):
        kind, j = i % 3, i // 3
        if kind == 0:
            y = _nsa_mixer(h, bsz, seq, j, nsa_w_in, nsa_cmp_pe, nsa_cmp_w1, nsa_cmp_w2, nsa_w_out)
        elif kind == 1:
            y = _rglru_mixer(h, bsz, seq, rg_w_in[j], rg_conv_w[j], rg_conv_b[j], rg_gate_w[j], rg_gate_b[j],
                             rg_lambda[j], rg_w_out[j])
        else:
            y = _hgrn2_mixer(h, bsz, seq, hg_w_in[j], hg_lb_logits, i, hg_norm_gain[j], hg_w_out[j])
        next_gain = pre_norm_gain[i + 1] if i + 1 < depth else None
        x2d, h = _postnorm_residual(x2d, y, post_norm_gain[i], next_gain)
    return x2d.reshape(bsz, seq, d)
```
